```python
import math
import jax, jax.numpy as jnp
from jax import lax
import numpy as np

D_MODEL = 1024
BATCH = 16
SEQ = 2048
DEPTH = 2
DEC_BATCH = 8
DEC_SEQ = 4096
PAST_LEN = 128

N_META = 16
GRID_W = 64
EPS = 1e-6

D_SSM = D_MODEL // 4
D_MLSTM = D_MODEL // 4
D_ATTN = D_MODEL // 2
D_MIX = D_SSM + D_MLSTM + D_ATTN

SSM_GROUP = 16
SSM_GROUPS = D_SSM // SSM_GROUP
SSM_STATE = 64

ML_HEADS = 4
ML_HEAD_DIM = D_MLSTM // ML_HEADS
ML_CHUNK = 64
ML_CONV = 3

ATT_HEADS = 8
ATT_KV_HEADS = 2
ATT_HEAD_DIM = D_ATTN // ATT_HEADS
ATT_KV_W = ATT_KV_HEADS * ATT_HEAD_DIM
ATT_BLOCK = 128
ROPE_AXIS = ATT_HEAD_DIM // 2
ROPE_THETA = 10000.0

D_FF = 2816
N_EXPERTS = 8
TOP_K = 2
N_DENSE = (DEPTH + 1) // 2
N_MOE = DEPTH // 2

IN_SPLITS = (D_SSM, D_MLSTM, D_MLSTM, D_MLSTM, D_MLSTM, 2 * ML_HEADS, 2 * ML_HEADS, D_ATTN, ATT_KV_W, ATT_KV_W)
D_IN = D_SSM + 4 * D_MLSTM + 4 * ML_HEADS + D_ATTN + 2 * ATT_KV_W

kernel_name = 'hymba_s5_mlstm_gqa_encoder'

f32 = jnp.float32


def rms_norm(x, g):
    xf = x.astype(f32)
    y = xf * lax.rsqrt(jnp.mean(xf * xf, -1, keepdims=True) + EPS)
    return (y * g.astype(f32)).astype(x.dtype)


def s5_direction(u, lam_re, lam_im, log_step, b_re, b_im, c_re, c_im):
    dt = jnp.exp(log_step.astype(f32))[:, None]
    lr = lam_re.astype(f32)
    li = lam_im.astype(f32)
    mag = jnp.exp(lr * dt)
    ab_re = mag * jnp.cos(li * dt)
    ab_im = mag * jnp.sin(li * dt)
    den = lr * lr + li * li
    nr = ab_re - 1.0
    ni = ab_im
    coef_re = (nr * lr + ni * li) / den
    coef_im = (ni * lr - nr * li) / den
    br = b_re.astype(f32)
    bi = b_im.astype(f32)
    bb_re = coef_re[..., None] * br - coef_im[..., None] * bi
    bb_im = coef_re[..., None] * bi + coef_im[..., None] * br
    uf = u.astype(f32)
    bu_re = jnp.einsum('gpc,blgc->blgp', bb_re, uf)
    bu_im = jnp.einsum('gpc,blgc->blgp', bb_im, uf)
    a_re = jnp.broadcast_to(ab_re, bu_re.shape)
    a_im = jnp.broadcast_to(ab_im, bu_im.shape)

    def combine(e1, e2):
        a1r, a1i, b1r, b1i = e1
        a2r, a2i, b2r, b2i = e2
        return (a2r * a1r - a2i * a1i,
                a2r * a1i + a2i * a1r,
                a2r * b1r - a2i * b1i + b2r,
                a2r * b1i + a2i * b1r + b2i)

    _, _, xr, xi = lax.associative_scan(combine, (a_re, a_im, bu_re, bu_im), axis=1)
    return (jnp.einsum('gcp,blgp->blgc', c_re.astype(f32), xr)
            - jnp.einsum('gcp,blgp->blgc', c_im.astype(f32), xi))


def s5_mixer(u, lam_re, lam_im, log_step, b_re, b_im, c_re, c_im, d_skip, w_glu):
    Bn, L, _ = u.shape
    ug = u.reshape(Bn, L, SSM_GROUPS, SSM_GROUP)
    y_f = s5_direction(ug, lam_re[0], lam_im[0], log_step[0], b_re[0], b_im[0], c_re[0], c_im[0])
    y_b = jnp.flip(s5_direction(jnp.flip(ug, 1), lam_re[1], lam_im[1], log_step[1],
                                b_re[1], b_im[1], c_re[1], c_im[1]), 1)
    y = (y_f + y_b).reshape(Bn, L, D_SSM) + d_skip.astype(f32) * u.astype(f32)
    y = jax.nn.gelu(y)
    a, g = jnp.split(y @ w_glu.astype(f32), 2, axis=-1)
    return a * jax.nn.sigmoid(g)


def mlstm_chunkwise(q, k, v, log_i, log_f):
    Bn, H, T, dh = q.shape
    nc = T // ML_CHUNK
    q = q.reshape(Bn, H, nc, ML_CHUNK, dh)
    k = k.reshape(Bn, H, nc, ML_CHUNK, dh)
    v = v.reshape(Bn, H, nc, ML_CHUNK, dh)
    li = log_i.reshape(Bn, H, nc, ML_CHUNK)
    lf = log_f.reshape(Bn, H, nc, ML_CHUNK)
    b = jnp.cumsum(lf, axis=-1)
    causal = jnp.tril(jnp.ones((ML_CHUNK, ML_CHUNK), dtype=bool))
    dmat = jnp.where(causal, b[..., :, None] - b[..., None, :] + li[..., None, :], -jnp.inf)
    b_last = b[..., -1]
    g = b_last[..., None] - b + li
    m_loc = jnp.max(g, axis=-1)
    w = jnp.exp(g - m_loc[..., None])
    c_loc = jnp.einsum('bhns,bhnsv,bhnsk->bhnvk', w, v, k)
    n_loc = jnp.einsum('bhns,bhnsk->bhnk', w, k)

    def step(carry, xs):
        c, n, m = carry
        bl, ml, cl, nl = xs
        m_new = jnp.maximum(bl + m, ml)
        s_old = jnp.exp(bl + m - m_new)
        s_loc = jnp.exp(ml - m_new)
        c_new = s_old[..., None, None] * c + s_loc[..., None, None] * cl
        n_new = s_old[..., None] * n + s_loc[..., None] * nl
        return (c_new, n_new, m_new), (c, n, m)

    init = (jnp.zeros((Bn, H, dh, dh), f32), jnp.zeros((Bn, H, dh), f32), jnp.zeros((Bn, H), f32))
    xs = (jnp.moveaxis(b_last, 2, 0), jnp.moveaxis(m_loc, 2, 0),
          jnp.moveaxis(c_loc, 2, 0), jnp.moveaxis(n_loc, 2, 0))
    _, (c_prev, n_prev, m_prev) = lax.scan(step, init, xs)
    c_prev = jnp.moveaxis(c_prev, 0, 2)
    n_prev = jnp.moveaxis(n_prev, 0, 2)
    m_prev = jnp.moveaxis(m_prev, 0, 2)

    a = b + m_prev[..., None]
    m_t = jnp.maximum(a, jnp.max(dmat, axis=-1))
    p = jnp.exp(dmat - m_t[..., None])
    s = jnp.einsum('bhntd,bhnsd->bhnts', q, k) * p
    e = jnp.exp(a - m_t)
    num = (jnp.einsum('bhnts,bhnsd->bhntd', s, v)
           + e[..., None] * jnp.einsum('bhnvk,bhntk->bhntv', c_prev, q))
    den = jnp.sum(s, axis=-1) + e * jnp.einsum('bhnk,bhntk->bhnt', n_prev, q)
    den = jnp.maximum(jnp.abs(den), jnp.exp(-m_t))
    h = num / den[..., None]
    return h.reshape(Bn, H, T, dh)


def centred_conv(x, w, b):
    half = ML_CONV // 2
    L = x.shape[1]
    xp = jnp.pad(x, ((0, 0), (half, half), (0, 0)))
    out = xp[:, 0:L] * w[0]
    for j in range(1, ML_CONV):
        out = out + xp[:, j:j + L] * w[j]
    return out + b


def mlstm_mixer(q, k, v, o, gi, gf, conv_w, conv_b, b_i, b_f, norm_g):
    Bn, L, _ = q.shape
    H = ML_HEADS
    qk = centred_conv(jnp.concatenate([q, k], -1).astype(f32), conv_w.astype(f32), conv_b.astype(f32))
    q, k = qk[..., :D_MLSTM], qk[..., D_MLSTM:]
    pad = ML_CHUNK - N_META
    T = pad + L

    def to_heads(t):
        t = t.astype(f32).reshape(Bn, L, H, ML_HEAD_DIM)
        t = jnp.pad(t, ((0, 0), (pad, 0), (0, 0), (0, 0)))
        return t.transpose(0, 2, 1, 3)

    qh = to_heads(q)
    kh = to_heads(k) * (ML_HEAD_DIM ** -0.5)
    vh = to_heads(v)
    valid = (jnp.arange(T) >= pad)[None, None, :]

    def to_gate(gt, bias):
        gt = gt.astype(f32) + bias.astype(f32).reshape(-1)
        gt = jnp.pad(gt, ((0, 0), (pad, 0), (0, 0)))
        return gt.transpose(0, 2, 1)

    log_i = jnp.where(valid, to_gate(gi, b_i), -jnp.inf)
    log_f = jnp.where(valid, jax.nn.log_sigmoid(to_gate(gf, b_f)), 0.0)
    h_fwd = mlstm_chunkwise(qh, kh, vh, log_i[:, :H], log_f[:, :H])
    rev = lambda t: jnp.flip(t, 2)
    h_bwd = rev(mlstm_chunkwise(rev(qh), rev(kh), rev(vh), rev(log_i[:, H:]), rev(log_f[:, H:])))
    h = (h_fwd + h_bwd)[:, :, pad:]
    h = h * lax.rsqrt(jnp.mean(h * h, -1, keepdims=True) + EPS) * norm_g.astype(f32).reshape(H, 1, ML_HEAD_DIM)
    h = h.transpose(0, 2, 1, 3).reshape(Bn, L, D_MLSTM)
    return jax.nn.sigmoid(o.astype(f32)) * h


def axial_rope_tables(n_tokens):
    rows = n_tokens // GRID_W
    row = jnp.concatenate([jnp.full((N_META,), -1.0, f32),
                           jnp.repeat(jnp.arange(rows, dtype=f32), GRID_W)])
    col = jnp.concatenate([jnp.arange(N_META, dtype=f32),
                           jnp.tile(jnp.arange(GRID_W, dtype=f32), rows)])
    inv = ROPE_THETA ** (-jnp.arange(0, ROPE_AXIS, 2, dtype=f32) / ROPE_AXIS)
    ar = row[:, None] * inv
    ac = col[:, None] * inv
    ang = jnp.concatenate([ar, ar, ac, ac], -1)
    return jnp.cos(ang), jnp.sin(ang)


def apply_rope(x, cos, sin):
    qd = ROPE_AXIS // 2
    x1 = x[..., :qd]
    x2 = x[..., qd:2 * qd]
    x3 = x[..., 2 * qd:3 * qd]
    x4 = x[..., 3 * qd:]
    rot = jnp.concatenate([-x2, x1, -x4, x3], -1)
    return x * cos[:, None, :] + rot * sin[:, None, :]


def attention_mixer(q, k, v, q_g, k_g, cos, sin):
    Bn, L, _ = q.shape
    grp = ATT_HEADS // ATT_KV_HEADS
    q = q.reshape(Bn, L, ATT_HEADS, ATT_HEAD_DIM)
    k = k.reshape(Bn, L, ATT_KV_HEADS, ATT_HEAD_DIM)
    v = v.reshape(Bn, L, ATT_KV_HEADS, ATT_HEAD_DIM)
    q = apply_rope(rms_norm(q, q_g), cos, sin) * (ATT_HEAD_DIM ** -0.5)
    k = apply_rope(rms_norm(k, k_g), cos, sin)
    q = q.reshape(Bn, L, ATT_KV_HEADS, grp, ATT_HEAD_DIM)

    def attend(qb):
        s = jnp.einsum('bqhgd,bkhd->bhgqk', qb, k, preferred_element_type=f32)
        p = jax.nn.softmax(s, axis=-1)
        return jnp.einsum('bhgqk,bkhd->bqhgd', p.astype(v.dtype), v, preferred_element_type=f32)

    out_meta = attend(q[:, :N_META]).reshape(Bn, N_META, D_ATTN)
    n_real = L - N_META
    nb = n_real // ATT_BLOCK
    qr = q[:, N_META:].reshape(Bn, nb, ATT_BLOCK, ATT_KV_HEADS, grp, ATT_HEAD_DIM).transpose(1, 0, 2, 3, 4, 5)
    out_real = lax.map(attend, qr)
    out_real = out_real.transpose(1, 0, 2, 3, 4, 5).reshape(Bn, n_real, D_ATTN)
    return jnp.concatenate([out_meta, out_real], axis=1)


def swiglu(x, w1, w3, w2):
    return (jax.nn.silu(x @ w1) * (x @ w3)) @ w2


def moe_swiglu(x, w_router, w1, w3, w2):
    Bn, L, D = x.shape
    xt = x.reshape(-1, D)
    logits = (xt @ w_router).astype(f32)
    top_v, top_i = lax.top_k(logits, TOP_K)
    gates = jax.nn.softmax(top_v, axis=-1)
    dense_gate = jnp.sum(jax.nn.one_hot(top_i, N_EXPERTS, dtype=f32) * gates[..., None], axis=1)
    y = jnp.zeros(xt.shape, f32)
    for e in range(N_EXPERTS):
        y = y + dense_gate[:, e:e + 1] * swiglu(xt, w1[e], w3[e], w2[e]).astype(f32)
    return y.reshape(Bn, L, D)


def trunk(x, meta_tokens, norm1_g, w_in, w_out,
          ssm_lam_re, ssm_lam_im, ssm_log_step, ssm_b_re, ssm_b_im, ssm_c_re, ssm_c_im, ssm_d, ssm_w_glu,
          ml_conv_w, ml_conv_b, ml_b_i, ml_b_f, ml_norm_g, att_q_g, att_k_g,
          norm2_g, ffn_w1, ffn_w3, ffn_w2, moe_router, moe_w1, moe_w3, moe_w2, final_g):
    Bn, N, D = x.shape
    h = jnp.concatenate([jnp.broadcast_to(meta_tokens.astype(x.dtype), (Bn, N_META, D)), x], axis=1)
    cos, sin = axial_rope_tables(N)
    split_at = [int(sum(IN_SPLITS[:j + 1])) for j in range(len(IN_SPLITS) - 1)]
    for l in range(DEPTH):
        xn = rms_norm(h, norm1_g[l])
        proj = xn @ w_in[l]
        (u_s, m_q, m_k, m_v, m_o, m_i, m_f, a_q, a_k, a_v) = jnp.split(proj, split_at, axis=-1)
        y_ssm = s5_mixer(u_s, ssm_lam_re[l], ssm_lam_im[l], ssm_log_step[l], ssm_b_re[l], ssm_b_im[l],
                         ssm_c_re[l], ssm_c_im[l], ssm_d[l], ssm_w_glu[l])
        y_ml = mlstm_mixer(m_q, m_k, m_v, m_o, m_i, m_f, ml_conv_w[l], ml_conv_b[l],
                           ml_b_i[l], ml_b_f[l], ml_norm_g[l])
        y_att = attention_mixer(a_q, a_k, a_v, att_q_g[l], att_k_g[l], cos, sin)
        mix = jnp.concatenate([y_ssm, y_ml, y_att], axis=-1).astype(h.dtype)
        h = h + mix @ w_out[l]
        xn = rms_norm(h, norm2_g[l])
        if l % 2 == 0:
            j = l // 2
            h = h + swiglu(xn, ffn_w1[j], ffn_w3[j], ffn_w2[j]).astype(h.dtype)
        else:
            j = l // 2
            h = h + moe_swiglu(xn, moe_router[j], moe_w1[j], moe_w3[j], moe_w2[j]).astype(h.dtype)
    h = rms_norm(h, final_g)
    return h[:, N_META:]


def setup_inputs(seed: int = 0) -> dict:
    key = jax.random.key(seed)
    ks = jax.random.split(key, 40)
    nrm = lambda i, shape, scale: jax.random.normal(ks[i], shape, f32) * scale
    G, P, C = SSM_GROUPS, SSM_STATE, SSM_GROUP
    lam_im0 = math.pi * jnp.arange(P, dtype=f32)
    return {
        'x_prompt': nrm(0, (BATCH, SEQ, D_MODEL), 1.0),
        'x_sample': nrm(1, (DEC_BATCH, DEC_SEQ, D_MODEL), 1.0),
        'meta_tokens': nrm(2, (N_META, D_MODEL), 1.0),
        'norm1_g': 1.0 + nrm(3, (DEPTH, D_MODEL), 0.01),
        'w_in': nrm(4, (DEPTH, D_MODEL, D_IN), D_MODEL ** -0.5),
        'w_out': nrm(5, (DEPTH, D_MIX, D_MODEL), 0.5 * D_MIX ** -0.5),
        'ssm_lam_re': -0.5 * jnp.exp(nrm(6, (DEPTH, 2, G, P), 0.05)),
        'ssm_lam_im': lam_im0 + nrm(7, (DEPTH, 2, G, P), 0.01),
        'ssm_log_step': jax.random.uniform(ks[8], (DEPTH, 2, G), f32, math.log(1e-3), math.log(1e-1)),
        'ssm_b_re': nrm(9, (DEPTH, 2, G, P, C), (2 * C) ** -0.5),
        'ssm_b_im': nrm(10, (DEPTH, 2, G, P, C), (2 * C) ** -0.5),
        'ssm_c_re': nrm(11, (DEPTH, 2, G, C, P), (2 * P) ** -0.5),
        'ssm_c_im': nrm(12, (DEPTH, 2, G, C, P), (2 * P) ** -0.5),
        'ssm_d': nrm(13, (DEPTH, D_SSM), 1.0),
        'ssm_w_glu': nrm(14, (DEPTH, D_SSM, 2 * D_SSM), D_SSM ** -0.5),
        'ml_conv_w': nrm(15, (DEPTH, ML_CONV, 2 * D_MLSTM), ML_CONV ** -0.5),
        'ml_conv_b': nrm(16, (DEPTH, 2 * D_MLSTM), 0.01),
        'ml_b_i': nrm(17, (DEPTH, 2, ML_HEADS), 0.1),
        'ml_b_f': jnp.linspace(3.0, 6.0, ML_HEADS, dtype=f32) + nrm(18, (DEPTH, 2, ML_HEADS), 0.1),
        'ml_norm_g': 1.0 + nrm(19, (DEPTH, D_MLSTM), 0.01),
        'att_q_g': 1.0 + nrm(20, (DEPTH, ATT_HEAD_DIM), 0.01),
        'att_k_g': 1.0 + nrm(21, (DEPTH, ATT_HEAD_DIM), 0.01),
        'norm2_g': 1.0 + nrm(22, (DEPTH, D_MODEL), 0.01),
        'ffn_w1': nrm(23, (N_DENSE, D_MODEL, D_FF), D_MODEL ** -0.5),
        'ffn_w3': nrm(24, (N_DENSE, D_MODEL, D_FF), D_MODEL ** -0.5),
        'ffn_w2': nrm(25, (N_DENSE, D_FF, D_MODEL), D_FF ** -0.5),
        'moe_router': nrm(26, (N_MOE, D_MODEL, N_EXPERTS), D_MODEL ** -0.5),
        'moe_w1': nrm(27, (N_MOE, N_EXPERTS, D_MODEL, D_FF), D_MODEL ** -0.5),
        'moe_w3': nrm(28, (N_MOE, N_EXPERTS, D_MODEL, D_FF), D_MODEL ** -0.5),
        'moe_w2': nrm(29, (N_MOE, N_EXPERTS, D_FF, D_MODEL), D_FF ** -0.5),
        'final_g': 1.0 + nrm(30, (D_MODEL,), 0.01),
    }


def reference(x_prompt, x_sample, meta_tokens, norm1_g, w_in, w_out,
              ssm_lam_re, ssm_lam_im, ssm_log_step, ssm_b_re, ssm_b_im, ssm_c_re, ssm_c_im, ssm_d, ssm_w_glu,
              ml_conv_w, ml_conv_b, ml_b_i, ml_b_f, ml_norm_g, att_q_g, att_k_g,
              norm2_g, ffn_w1, ffn_w3, ffn_w2, moe_router, moe_w1, moe_w3, moe_w2, final_g):
    run = functools_partial_free = None
    y_prompt = trunk(x_prompt, meta_tokens, norm1_g, w_in, w_out,
                     ssm_lam_re, ssm_lam_im, ssm_log_step, ssm_b_re, ssm_b_im, ssm_c_re, ssm_c_im, ssm_d, ssm_w_glu,
                     ml_conv_w, ml_conv_b, ml_b_i, ml_b_f, ml_norm_g, att_q_g, att_k_g,
                     norm2_g, ffn_w1, ffn_w3, ffn_w2, moe_router, moe_w1, moe_w3, moe_w2, final_g)
    y_sample = trunk(x_sample, meta_tokens, norm1_g, w_in, w_out,
                     ssm_lam_re, ssm_lam_im, ssm_log_step, ssm_b_re, ssm_b_im, ssm_c_re, ssm_c_im, ssm_d, ssm_w_glu,
                     ml_conv_w, ml_conv_b, ml_b_i, ml_b_f, ml_norm_g, att_q_g, att_k_g,
                     norm2_g, ffn_w1, ffn_w3, ffn_w2, moe_router, moe_w1, moe_w3, moe_w2, final_g)
    return (y_prompt, y_sample)
```

```python
import functools
import math

import jax
import jax.numpy as jnp
from jax import lax
from jax.experimental import pallas as pl
from jax.experimental.pallas import tpu as pltpu

f32 = jnp.float32
bf16 = jnp.bfloat16

D_MODEL = 1024
N_META = 16
GRID_W = 64
EPS = 1e-6
D_SSM = 256
D_MLSTM = 256
D_ATTN = 512
SSM_GROUP = 16
SSM_GROUPS = 16
SSM_STATE = 64
SSM_LANES = SSM_GROUPS * SSM_STATE
ML_HEADS = 4
ML_HEAD_DIM = 64
ATT_HEADS = 8
ATT_KV_HEADS = 2
ATT_HEAD_DIM = 64
ATT_KV_W = 128
ROPE_AXIS = 32
ROPE_THETA = 10000.0
D_FF = 2816
N_EXPERTS = 8

LANE = 128
HEAD_ROWS = 128
PAD_FRONT = HEAD_ROWS - N_META
ML_CHUNK = 128
S5_CHUNK = 64
ATT_TQ = 128
FF_HALF = D_FF // 2
NEG = -1e30
VMEM_LIMIT = 48 * 1024 * 1024

C_U = 0
C_MQK = 256
C_MV = 768
C_MO = 1024
C_G = 1280
C_AQ = 1408
C_AK = 1920
C_AV = 2048
D_INP = 2176


def _dot(a, b):
    return jnp.dot(a, b, preferred_element_type=f32)


def _dot_t(a, b):
    return lax.dot_general(a, b, (((1,), (1,)), ((), ())), preferred_element_type=f32)


def _split2_dot(x, m):
    hi = x.astype(bf16)
    lo = (x - hi.astype(f32)).astype(bf16)
    return _dot(hi, m) + _dot(lo, m)


def _split3(x):
    hi = x.astype(bf16)
    r = x - hi.astype(f32)
    mid = r.astype(bf16)
    lo = (r - mid.astype(f32)).astype(bf16)
    return hi, mid, lo


def _sigmoid(x):
    return 1.0 / (1.0 + jnp.exp(-x))


def _rope(x, cos, sin_a, sin_b):
    w = x.shape[-1]
    xl = pltpu.roll(x, w - 16, 1)
    xr = pltpu.roll(x, 16, 1)
    return x * cos + xl * sin_a + xr * sin_b


def _in_proj_kernel(h_ref, g_ref, w_ref, cos_ref, sa_ref, sb_ref, qg_ref, kg_ref, seg_ref,
                    u_ref, mqk_ref, mv_ref, mo_ref, gt_ref, aq_ref, ak_ref, av_ref):
    h = h_ref[...]
    xn = h * lax.rsqrt(jnp.mean(h * h, axis=-1, keepdims=True) + EPS) * g_ref[...]
    p = _dot(xn.astype(bf16), w_ref[...])
    u_ref[...] = p[:, C_U:C_MQK]
    mqk_ref[...] = p[:, C_MQK:C_MV]
    mv_ref[...] = p[:, C_MV:C_MO].astype(bf16)
    mo_ref[...] = p[:, C_MO:C_G].astype(bf16)
    gt_ref[...] = p[:, C_G:C_AQ]
    cos = cos_ref[...]
    sa = sa_ref[...]
    sb = sb_ref[...]
    seg = seg_ref[...]
    q = p[:, C_AQ:C_AK]
    qn = q * lax.rsqrt(_split2_dot(q * q, seg) + EPS) * qg_ref[...]
    rep = lambda t: jnp.concatenate([t] * (D_ATTN // LANE), axis=1)
    aq_ref[...] = (_rope(qn, rep(cos), rep(sa), rep(sb)) * (ATT_HEAD_DIM ** -0.5)).astype(bf16)
    k = p[:, C_AK:C_AV]
    kn = k * lax.rsqrt(_split2_dot(k * k, seg[:ATT_KV_W, :ATT_KV_W]) + EPS) * kg_ref[...]
    ak_ref[...] = _rope(kn, cos, sa, sb).astype(bf16)
    av_ref[...] = p[:, C_AV:D_INP].astype(bf16)


def _in_proj(h, g, w, cos, sa, sb, qg, kg, seg, tm):
    B, Lp, D = h.shape
    nt = Lp // tm
    row = lambda w_: pl.BlockSpec((None, tm, w_), lambda b, i: (b, i, 0))
    full = lambda a: pl.BlockSpec(a.shape, lambda b, i: (0,) * a.ndim)
    tab = pl.BlockSpec((tm, LANE), lambda b, i: (i, 0))
    return pl.pallas_call(
        _in_proj_kernel,
        grid=(B, nt),
        in_specs=[row(D), full(g), full(w), tab, tab, tab, full(qg), full(kg), full(seg)],
        out_specs=[
            pl.BlockSpec((tm, D_SSM), lambda b, i: (i, b)),
            row(2 * D_MLSTM), row(D_MLSTM), row(D_MLSTM), row(LANE),
            row(D_ATTN), row(ATT_KV_W), row(ATT_KV_W),
        ],
        out_shape=[
            jax.ShapeDtypeStruct((Lp, B * D_SSM), f32),
            jax.ShapeDtypeStruct((B, Lp, 2 * D_MLSTM), f32),
            jax.ShapeDtypeStruct((B, Lp, D_MLSTM), bf16),
            jax.ShapeDtypeStruct((B, Lp, D_MLSTM), bf16),
            jax.ShapeDtypeStruct((B, Lp, LANE), f32),
            jax.ShapeDtypeStruct((B, Lp, D_ATTN), bf16),
            jax.ShapeDtypeStruct((B, Lp, ATT_KV_W), bf16),
            jax.ShapeDtypeStruct((B, Lp, ATT_KV_W), bf16),
        ],
        compiler_params=pltpu.CompilerParams(
            dimension_semantics=("parallel", "arbitrary"), vmem_limit_bytes=VMEM_LIMIT),
        name="in_proj",
    )(h, g, w, cos, sa, sb, qg, kg, seg)


def _s5_kernel(u_ref, bm_ref, cm_ref, are_ref, aim_ref, y_ref, x_ref, st_ref, *, tc, nb):
    d = pl.program_id(0)
    c = pl.program_id(1)

    @pl.when(c == 0)
    def _():
        st_ref[...] = jnp.zeros_like(st_ref)

    u = u_ref[...].reshape(tc * nb, D_SSM).astype(bf16)
    x_ref[...] = _dot(u, bm_ref[...])
    lw = 512
    for lb in range(SSM_LANES // lw):
        re_sl = pl.ds(lb * lw, lw)
        im_sl = pl.ds(SSM_LANES + lb * lw, lw)
        a_re = jnp.broadcast_to(are_ref[:, lb * lw:(lb + 1) * lw], (nb, lw))
        a_im = jnp.broadcast_to(aim_ref[:, lb * lw:(lb + 1) * lw], (nb, lw))

        def step(t, carry):
            xr, xi = carry
            tt = d * (tc - 1 - t) + (1 - d) * t
            r = pl.multiple_of(tt * nb, 8)
            nr = a_re * xr - a_im * xi + x_ref[pl.ds(r, nb), re_sl]
            ni = a_re * xi + a_im * xr + x_ref[pl.ds(r, nb), im_sl]
            x_ref[pl.ds(r, nb), re_sl] = nr
            x_ref[pl.ds(r, nb), im_sl] = ni
            return nr, ni

        xr, xi = lax.fori_loop(0, tc, step, (st_ref[:, re_sl], st_ref[:, im_sl]), unroll=4)
        st_ref[:, re_sl] = xr
        st_ref[:, im_sl] = xi
    y = _dot(x_ref[...].astype(bf16), cm_ref[...])
    y_ref[...] = y.reshape(tc, nb, D_SSM)


def _s5_scan(u_t, bmat, cmat, a_re, a_im, nb):
    Lp = u_t.shape[0]
    tc = S5_CHUNK
    nc = Lp // tc
    cidx = lambda d, c: d * (nc - 1 - c) + (1 - d) * c
    u3 = u_t.reshape(Lp, nb, D_SSM)
    return pl.pallas_call(
        functools.partial(_s5_kernel, tc=tc, nb=nb),
        grid=(2, nc),
        in_specs=[
            pl.BlockSpec((tc, nb, D_SSM), lambda d, c: (cidx(d, c), 0, 0)),
            pl.BlockSpec((None, D_SSM, 2 * SSM_LANES), lambda d, c: (d, 0, 0)),
            pl.BlockSpec((None, 2 * SSM_LANES, D_SSM), lambda d, c: (d, 0, 0)),
            pl.BlockSpec((None, 1, SSM_LANES), lambda d, c: (d, 0, 0)),
            pl.BlockSpec((None, 1, SSM_LANES), lambda d, c: (d, 0, 0)),
        ],
        out_specs=pl.BlockSpec((None, tc, nb, D_SSM), lambda d, c: (d, cidx(d, c), 0, 0)),
        out_shape=jax.ShapeDtypeStruct((2, Lp, nb, D_SSM), f32),
        scratch_shapes=[pltpu.VMEM((tc * nb, 2 * SSM_LANES), f32),
                        pltpu.VMEM((nb, 2 * SSM_LANES), f32)],
        compiler_params=pltpu.CompilerParams(
            dimension_semantics=("arbitrary", "arbitrary"), vmem_limit_bytes=VMEM_LIMIT),
        name="s5_scan",
    )(u3, bmat, cmat, a_re, a_im)


def _conv3(x, prev8, next8, w_ref, b_ref, first, last):
    n = x.shape[0]
    rows = lax.broadcasted_iota(jnp.int32, x.shape, 0)
    pv = jnp.where(first, 0.0, prev8[7:8, :])
    nx = jnp.where(last, 0.0, next8[0:1, :])
    xm = jnp.where(rows == 0, pv, pltpu.roll(x, 1, 0))
    xp = jnp.where(rows == n - 1, nx, pltpu.roll(x, n - 1, 0))
    return xm * w_ref[0:1, :] + x * w_ref[1:2, :] + xp * w_ref[2:3, :] + b_ref[...]


def _mlstm_dir(qk, v, g, chunk, bias, ct_ref, n_ref, m_ref, h_ref, *, reverse):
    lc = ML_CHUNK
    H = ML_HEADS
    off = H if reverse else 0
    r_i = lax.broadcasted_iota(jnp.int32, (lc, lc), 0)
    c_i = lax.broadcasted_iota(jnp.int32, (lc, lc), 1)
    keep = (c_i >= r_i) if reverse else (c_i <= r_i)
    tri = keep.astype(bf16)
    gb = g + bias
    pos = chunk * lc + lax.broadcasted_iota(jnp.int32, (lc, LANE), 0)
    valid = pos >= PAD_FRONT
    li = jnp.where(valid, gb, NEG)
    z = gb
    lf = jnp.where(valid, jnp.minimum(z, 0.0) - jnp.log(1.0 + jnp.exp(-jnp.abs(z))), 0.0)
    hi, mid, lo = _split3(lf)
    bcol = _dot(tri, hi) + _dot(tri, mid) + _dot(tri, lo)
    li_t = li.T
    b_t = bcol.T
    q_all = qk[:, :D_MLSTM].astype(bf16)
    k_all = (qk[:, D_MLSTM:] * (ML_HEAD_DIM ** -0.5)).astype(bf16)
    e_row = 0 if reverse else lc - 1
    outs = []
    for hh in range(H):
        ci = off + hh
        cf = 2 * H + off + hh
        sl = slice(hh * ML_HEAD_DIM, (hh + 1) * ML_HEAD_DIM)
        q = q_all[:, sl]
        k = k_all[:, sl]
        vv = v[:, sl]
        bc = bcol[:, cf:cf + 1]
        br = b_t[cf:cf + 1, :]
        lir = li_t[ci:ci + 1, :]
        lic = li[:, ci:ci + 1]
        m_prev = m_ref[ci][:, 0:1]
        dm = jnp.where(keep, bc - br + lir, NEG)
        a = bc + m_prev
        m_t = jnp.maximum(a, jnp.max(dm, axis=1, keepdims=True))
        pw = jnp.exp(dm - m_t)
        s = _dot_t(q, k) * pw
        e = jnp.exp(a - m_t)
        qf = q.astype(f32)
        num = _dot(s.astype(bf16), vv) + e * _dot(q, ct_ref[ci].astype(bf16))
        den = jnp.sum(s, axis=1, keepdims=True) + e * jnp.sum(qf * n_ref[ci], axis=1, keepdims=True)
        den = jnp.maximum(jnp.abs(den), jnp.exp(-m_t))
        outs.append(num / den)
        b_end = br[:, e_row:e_row + 1]
        g_row = b_end - br + lir
        g_col = b_end - bc + lic
        m_loc = jnp.max(g_row, axis=1, keepdims=True)
        kw = (k.astype(f32) * jnp.exp(g_col - m_loc))
        ct_loc = lax.dot_general(kw.astype(bf16), vv, (((0,), (0,)), ((), ())),
                                 preferred_element_type=f32)
        n_loc = jnp.sum(kw, axis=0, keepdims=True)
        m_new = jnp.maximum(b_end + m_prev, m_loc)
        s_old = jnp.exp(b_end + m_prev - m_new)
        s_loc = jnp.exp(m_loc - m_new)
        ct_ref[ci] = s_old * ct_ref[ci] + s_loc * ct_loc
        n_ref[ci] = s_old * n_ref[ci] + s_loc * n_loc
        m_ref[ci] = jnp.broadcast_to(m_new, (1, LANE))
    h_ref[...] = jnp.concatenate(outs, axis=1)


def _mlstm_kernel(qkf_ref, qkfp_ref, qkfn_ref, vf_ref, gf_ref,
                  qkb_ref, qkbp_ref, qkbn_ref, vb_ref, gb_ref,
                  cw_ref, cb_ref, bias_ref, hf_ref, hb_ref, ct_ref, n_ref, m_ref, *, nc):
    c = pl.program_id(1)

    @pl.when(c == 0)
    def _():
        ct_ref[...] = jnp.zeros_like(ct_ref)
        n_ref[...] = jnp.zeros_like(n_ref)
        m_ref[...] = jnp.zeros_like(m_ref)

    cb = nc - 1 - c
    bias = bias_ref[...]
    qk_f = _conv3(qkf_ref[...], qkfp_ref[...], qkfn_ref[...], cw_ref, cb_ref, c == 0, c == nc - 1)
    _mlstm_dir(qk_f, vf_ref[...], gf_ref[...], c, bias, ct_ref, n_ref, m_ref, hf_ref, reverse=False)
    qk_b = _conv3(qkb_ref[...], qkbp_ref[...], qkbn_ref[...], cw_ref, cb_ref, cb == 0, cb == nc - 1)
    _mlstm_dir(qk_b, vb_ref[...], gb_ref[...], cb, bias, ct_ref, n_ref, m_ref, hb_ref, reverse=True)


def _mlstm(mqk, mv, gates, conv_w, conv_b, bias):
    B, Lp, _ = mqk.shape
    lc = ML_CHUNK
    nc = Lp // lc
    n8 = Lp // 8
    per = lc // 8
    fwd = lambda b, c: (b, c, 0)
    bwd = lambda b, c: (b, nc - 1 - c, 0)
    prev = lambda f: (lambda b, c: (b, jnp.maximum(f(b, c)[1] * per - 1, 0), 0))
    nxt = lambda f: (lambda b, c: (b, jnp.minimum((f(b, c)[1] + 1) * per, n8 - 1), 0))
    full = lambda a: pl.BlockSpec(a.shape, lambda b, c: (0,) * a.ndim)
    w2 = 2 * D_MLSTM

    def specs(f):
        return [pl.BlockSpec((None, lc, w2), f), pl.BlockSpec((None, 8, w2), prev(f)),
                pl.BlockSpec((None, 8, w2), nxt(f)), pl.BlockSpec((None, lc, D_MLSTM), f),
                pl.BlockSpec((None, lc, LANE), f)]

    return pl.pallas_call(
        functools.partial(_mlstm_kernel, nc=nc),
        grid=(B, nc),
        in_specs=specs(fwd) + specs(bwd) + [full(conv_w), full(conv_b), full(bias)],
        out_specs=[pl.BlockSpec((None, lc, D_MLSTM), fwd), pl.BlockSpec((None, lc, D_MLSTM), bwd)],
        out_shape=[jax.ShapeDtypeStruct((B, Lp, D_MLSTM), f32)] * 2,
        scratch_shapes=[pltpu.VMEM((2 * ML_HEADS, ML_HEAD_DIM, ML_HEAD_DIM), f32),
                        pltpu.VMEM((2 * ML_HEADS, 1, ML_HEAD_DIM), f32),
                        pltpu.VMEM((2 * ML_HEADS, 1, LANE), f32)],
        compiler_params=pltpu.CompilerParams(
            dimension_semantics=("parallel", "arbitrary"), vmem_limit_bytes=VMEM_LIMIT),
        name="mlstm",
    )(mqk, mqk, mqk, mv, gates, mqk, mqk, mqk, mv, gates, conv_w, conv_b, bias)


def _attn_kernel(q_ref, k_ref, v_ref, kb_ref, o_ref):
    tq = q_ref.shape[0]
    grp = ATT_HEADS // ATT_KV_HEADS
    q = q_ref[...]
    kb = kb_ref[...]
    outs = []
    for g in range(ATT_KV_HEADS):
        kh = k_ref[:, g * ATT_HEAD_DIM:(g + 1) * ATT_HEAD_DIM]
        vh = v_ref[:, g * ATT_HEAD_DIM:(g + 1) * ATT_HEAD_DIM]
        q4 = jnp.concatenate(
            [q[:, (g * grp + j) * ATT_HEAD_DIM:(g * grp + j + 1) * ATT_HEAD_DIM] for j in range(grp)], axis=0)
        s = _dot_t(q4, kh) + kb
        m = jnp.max(s, axis=1, keepdims=True)
        p = jnp.exp(s - m)
        l = jnp.sum(p, axis=1, keepdims=True)
        o = _dot(p.astype(bf16), vh) / l
        outs.extend(o[j * tq:(j + 1) * tq] for j in range(grp))
    o_ref[...] = jnp.concatenate(outs, axis=1).astype(bf16)


def _attention(aq, ak, av, kbias):
    B, Lp, _ = aq.shape
    tq = ATT_TQ
    return pl.pallas_call(
        _attn_kernel,
        grid=(B, Lp // tq),
        in_specs=[pl.BlockSpec((None, tq, D_ATTN), lambda b, i: (b, i, 0)),
                  pl.BlockSpec((None, Lp, ATT_KV_W), lambda b, i: (b, 0, 0)),
                  pl.BlockSpec((None, Lp, ATT_KV_W), lambda b, i: (b, 0, 0)),
                  pl.BlockSpec((1, Lp), lambda b, i: (0, 0))],
        out_specs=pl.BlockSpec((None, tq, D_ATTN), lambda b, i: (b, i, 0)),
        out_shape=jax.ShapeDtypeStruct((B, Lp, D_ATTN), bf16),
        compiler_params=pltpu.CompilerParams(
            dimension_semantics=("parallel", "arbitrary"), vmem_limit_bytes=VMEM_LIMIT),
        name="attention",
    )(aq, ak, av, kbias)


def _out_proj_kernel(h_ref, ys_ref, u_ref, hf_ref, hb_ref, mo_ref, att_ref,
                     dsk_ref, wglu_ref, mlg_ref, seg_ref, wout_ref, g2_ref, wr_ref,
                     hn_ref, xn_ref, gate_ref, *, tm):
    i = pl.program_id(1)
    u = u_ref[...]
    y = ys_ref[0] + ys_ref[1] + dsk_ref[...] * u
    y = 0.5 * y * (1.0 + jnp.tanh(math.sqrt(2.0 / math.pi) * (y + 0.044715 * (y * y * y))))
    ag = _dot(y.astype(bf16), wglu_ref[...])
    y_ssm = ag[:, :D_SSM] * _sigmoid(ag[:, D_SSM:])
    hm = hf_ref[...] + hb_ref[...]
    hn = hm * lax.rsqrt(_split2_dot(hm * hm, seg_ref[...]) + EPS) * mlg_ref[...]
    y_ml = _sigmoid(mo_ref[...].astype(f32)) * hn
    acc = (_dot(y_ssm.astype(bf16), wout_ref[0:D_SSM, :])
           + _dot(y_ml.astype(bf16), wout_ref[D_SSM:D_SSM + D_MLSTM, :])
           + _dot(att_ref[...], wout_ref[D_SSM + D_MLSTM:, :]))
    pos = i * tm + lax.broadcasted_iota(jnp.int32, (tm, 1), 0)
    h_new = jnp.where(pos >= PAD_FRONT, h_ref[...] + acc, 0.0)
    hn_ref[...] = h_new
    xn = h_new * lax.rsqrt(jnp.mean(h_new * h_new, axis=-1, keepdims=True) + EPS) * g2_ref[...]
    xn_ref[...] = xn.astype(bf16)
    logits = jnp.dot(xn, wr_ref[...], preferred_element_type=f32, precision=lax.Precision.HIGHEST)
    lane = lax.broadcasted_iota(jnp.int32, logits.shape, 1).astype(f32)
    lg = jnp.where(lane < N_EXPERTS, logits, NEG)
    v1 = jnp.max(lg, axis=1, keepdims=True)
    i1 = jnp.min(jnp.where(lg == v1, lane, float(LANE)), axis=1, keepdims=True)
    lg2 = jnp.where(lane == i1, NEG, lg)
    v2 = jnp.max(lg2, axis=1, keepdims=True)
    i2 = jnp.min(jnp.where(lg2 == v2, lane, float(LANE)), axis=1, keepdims=True)
    g1 = 1.0 / (1.0 + jnp.exp(v2 - v1))
    gate_ref[...] = jnp.where(lane == i1, g1, 0.0) + jnp.where(lane == i2, 1.0 - g1, 0.0)


def _out_proj(h, ys, u_t, hf, hb, mo, att, dsk, wglu, mlg, seg, wout, g2, wr, tm):
    B, Lp, D = h.shape
    row = lambda w_: pl.BlockSpec((None, tm, w_), lambda b, i: (b, i, 0))
    full = lambda a: pl.BlockSpec(a.shape, lambda b, i: (0,) * a.ndim)
    ys2 = ys.reshape(2, Lp, B * D_SSM)
    return pl.pallas_call(
        functools.partial(_out_proj_kernel, tm=tm),
        grid=(B, Lp // tm),
        in_specs=[row(D),
                  pl.BlockSpec((2, tm, D_SSM), lambda b, i: (0, i, b)),
                  pl.BlockSpec((tm, D_SSM), lambda b, i: (i, b)),
                  row(D_MLSTM), row(D_MLSTM), row(D_MLSTM), row(D_ATTN),
                  full(dsk), full(wglu), full(mlg), full(seg), full(wout), full(g2), full(wr)],
        out_specs=[row(D), row(D), row(LANE)],
        out_shape=[jax.ShapeDtypeStruct((B, Lp, D), f32),
                   jax.ShapeDtypeStruct((B, Lp, D), bf16),
                   jax.ShapeDtypeStruct((B, Lp, LANE), f32)],
        compiler_params=pltpu.CompilerParams(
            dimension_semantics=("parallel", "arbitrary"), vmem_limit_bytes=VMEM_LIMIT),
        name="out_proj",
    )(h, ys2, u_t, hf, hb, mo, att, dsk, wglu, mlg, seg, wout, g2, wr)


def _swiglu_part(x, w1, w3, w2):
    a = _dot(x, w1)
    b = _dot(x, w3)
    return _dot((a * _sigmoid(a) * b).astype(bf16), w2)


def _final_norm(h, gf):
    return h * lax.rsqrt(jnp.mean(h * h, axis=-1, keepdims=True) + EPS) * gf


def _ffn_kernel(x_ref, h_ref, w1_ref, w3_ref, w2_ref, gf_ref, o_ref, *, final):
    j = pl.program_id(1)
    part = _swiglu_part(x_ref[...], w1_ref[...], w3_ref[...], w2_ref[...])

    @pl.when(j == 0)
    def _():
        o_ref[...] = h_ref[...] + part

    @pl.when(j > 0)
    def _():
        o_ref[...] += part

    if final:
        @pl.when(j == pl.num_programs(1) - 1)
        def _():
            o_ref[...] = _final_norm(o_ref[...], gf_ref[...])


def _ffn(x, h, w1, w3, w2, gf, tm, final):
    T, D = h.shape
    nj = D_FF // FF_HALF
    return pl.pallas_call(
        functools.partial(_ffn_kernel, final=final),
        grid=(T // tm, nj),
        in_specs=[pl.BlockSpec((tm, D), lambda i, j: (i, 0)),
                  pl.BlockSpec((tm, D), lambda i, j: (i, 0)),
                  pl.BlockSpec((D, FF_HALF), lambda i, j: (0, j)),
                  pl.BlockSpec((D, FF_HALF), lambda i, j: (0, j)),
                  pl.BlockSpec((FF_HALF, D), lambda i, j: (j, 0)),
                  pl.BlockSpec((1, D), lambda i, j: (0, 0))],
        out_specs=pl.BlockSpec((tm, D), lambda i, j: (i, 0)),
        out_shape=jax.ShapeDtypeStruct((T, D), f32),
        compiler_params=pltpu.CompilerParams(
            dimension_semantics=("parallel", "arbitrary"), vmem_limit_bytes=VMEM_LIMIT),
        name="ffn",
    )(x, h, w1, w3, w2, gf)


def _moe_kernel(x_ref, h_ref, gate_ref, w1_ref, w3_ref, w2_ref, gf_ref, o_ref, *, final):
    e = pl.program_id(1)
    j = pl.program_id(2)
    part = _swiglu_part(x_ref[...], w1_ref[...], w3_ref[...], w2_ref[...])
    gates = gate_ref[...]
    lane = lax.broadcasted_iota(jnp.int32, gates.shape, 1)
    ge = jnp.sum(jnp.where(lane == e, gates, 0.0), axis=1, keepdims=True)
    first = jnp.logical_and(e == 0, j == 0)

    @pl.when(first)
    def _():
        o_ref[...] = h_ref[...] + ge * part

    @pl.when(jnp.logical_not(first))
    def _():
        o_ref[...] += ge * part

    if final:
        @pl.when(jnp.logical_and(e == pl.num_programs(1) - 1, j == pl.num_programs(2) - 1))
        def _():
            o_ref[...] = _final_norm(o_ref[...], gf_ref[...])


def _moe(x, h, gates, w1, w3, w2, gf, tm, final):
    T, D = h.shape
    nj = D_FF // FF_HALF
    return pl.pallas_call(
        functools.partial(_moe_kernel, final=final),
        grid=(T // tm, N_EXPERTS, nj),
        in_specs=[pl.BlockSpec((tm, D), lambda i, e, j: (i, 0)),
                  pl.BlockSpec((tm, D), lambda i, e, j: (i, 0)),
                  pl.BlockSpec((tm, LANE), lambda i, e, j: (i, 0)),
                  pl.BlockSpec((None, D, FF_HALF), lambda i, e, j: (e, 0, j)),
                  pl.BlockSpec((None, D, FF_HALF), lambda i, e, j: (e, 0, j)),
                  pl.BlockSpec((None, FF_HALF, D), lambda i, e, j: (e, j, 0)),
                  pl.BlockSpec((1, D), lambda i, e, j: (0, 0))],
        out_specs=pl.BlockSpec((tm, D), lambda i, e, j: (i, 0)),
        out_shape=jax.ShapeDtypeStruct((T, D), f32),
        compiler_params=pltpu.CompilerParams(
            dimension_semantics=("parallel", "arbitrary", "arbitrary"), vmem_limit_bytes=VMEM_LIMIT),
        name="moe",
    )(x, h, gates, w1, w3, w2, gf)


def _s5_discretise(lam_re, lam_im, log_step, b_re, b_im, c_re, c_im):
    G, P, C = SSM_GROUPS, SSM_STATE, SSM_GROUP
    dt = jnp.exp(log_step)[..., None]
    mag = jnp.exp(lam_re * dt)
    ab_re = mag * jnp.cos(lam_im * dt)
    ab_im = mag * jnp.sin(lam_im * dt)
    den = lam_re * lam_re + lam_im * lam_im
    nr = ab_re - 1.0
    ni = ab_im
    coef_re = (nr * lam_re + ni * lam_im) / den
    coef_im = (ni * lam_re - nr * lam_im) / den
    bb_re = coef_re[..., None] * b_re - coef_im[..., None] * b_im
    bb_im = coef_re[..., None] * b_im + coef_im[..., None] * b_re
    eye = jnp.eye(G, dtype=f32)

    def bdiag_in(bb):
        return jnp.einsum('dgpc,gh->dgchp', bb, eye).reshape(2, G * C, G * P)

    def bdiag_out(cc):
        return jnp.einsum('dgcp,gh->dgphc', cc, eye).reshape(2, G * P, G * C)

    bmat = jnp.concatenate([bdiag_in(bb_re), bdiag_in(bb_im)], axis=2).astype(bf16)
    cmat = jnp.concatenate([bdiag_out(c_re), -bdiag_out(c_im)], axis=1).astype(bf16)
    return bmat, cmat, ab_re.reshape(2, 1, G * P), ab_im.reshape(2, 1, G * P)


def _rope_tables(n_tokens):
    rows = n_tokens // GRID_W
    row = jnp.concatenate([jnp.zeros((PAD_FRONT,), f32), jnp.full((N_META,), -1.0, f32),
                           jnp.repeat(jnp.arange(rows, dtype=f32), GRID_W)])
    col = jnp.concatenate([jnp.zeros((PAD_FRONT,), f32), jnp.arange(N_META, dtype=f32),
                           jnp.tile(jnp.arange(GRID_W, dtype=f32), rows)])
    inv = ROPE_THETA ** (-jnp.arange(0, ROPE_AXIS, 2, dtype=f32) / ROPE_AXIS)
    ar = row[:, None] * inv
    ac = col[:, None] * inv
    ang = jnp.concatenate([ar, ar, ac, ac], -1)
    cos, sin = jnp.cos(ang), jnp.sin(ang)
    first_half = (jnp.arange(ATT_HEAD_DIM) % ROPE_AXIS) < (ROPE_AXIS // 2)
    sa = jnp.where(first_half, -sin, 0.0)
    sb = jnp.where(first_half, 0.0, sin)
    two = lambda t: jnp.concatenate([t, t], axis=1)
    return two(cos), two(sa), two(sb)


def _seg_matrix(width, seg):
    idx = jnp.arange(width) // seg
    return (idx[:, None] == idx[None, :]).astype(f32).astype(bf16) * jnp.asarray(1.0 / seg, bf16)


def _row_tile(lp):
    best = 16
    for t in range(16, lp + 1, 16):
        if lp % t == 0 and abs(t - 512) < abs(best - 512):
            best = t
    return best


def _prep(meta_tokens, norm1_g, w_in, w_out,
          ssm_lam_re, ssm_lam_im, ssm_log_step, ssm_b_re, ssm_b_im, ssm_c_re, ssm_c_im, ssm_d, ssm_w_glu,
          ml_conv_w, ml_conv_b, ml_b_i, ml_b_f, ml_norm_g, att_q_g, att_k_g,
          norm2_g, ffn_w1, ffn_w3, ffn_w2, moe_router, moe_w1, moe_w3, moe_w2, final_g):
    depth = w_in.shape[0]
    layers = []
    for l in range(depth):
        w = w_in[l]
        wg = jnp.pad(w[:, 1280:1296], ((0, 0), (0, LANE - 16)))
        w_cat = jnp.concatenate([w[:, 0:1280], wg, w[:, 1296:]], axis=1).astype(bf16)
        bmat, cmat, a_re, a_im = _s5_discretise(ssm_lam_re[l], ssm_lam_im[l], ssm_log_step[l],
                                                ssm_b_re[l], ssm_b_im[l], ssm_c_re[l], ssm_c_im[l])
        gate_bias = jnp.pad(jnp.concatenate([ml_b_i[l].reshape(-1), ml_b_f[l].reshape(-1)]),
                            (0, LANE - 4 * ML_HEADS)).reshape(1, LANE)
        lay = dict(
            g1=norm1_g[l].reshape(1, -1), w_cat=w_cat,
            qg=jnp.tile(att_q_g[l], ATT_HEADS).reshape(1, -1),
            kg=jnp.tile(att_k_g[l], ATT_KV_HEADS).reshape(1, -1),
            bmat=bmat, cmat=cmat, a_re=a_re, a_im=a_im,
            dsk=ssm_d[l].reshape(1, -1), wglu=ssm_w_glu[l].astype(bf16),
            conv_w=ml_conv_w[l], conv_b=ml_conv_b[l].reshape(1, -1), gate_bias=gate_bias,
            mlg=ml_norm_g[l].reshape(1, -1), wout=w_out[l].astype(bf16),
            g2=norm2_g[l].reshape(1, -1),
        )
        j = l // 2
        if l % 2 == 0:
            lay.update(moe=False, wr=jnp.zeros((D_MODEL, LANE), f32),
                       w1=ffn_w1[j].astype(bf16), w3=ffn_w3[j].astype(bf16), w2=ffn_w2[j].astype(bf16))
        else:
            lay.update(moe=True, wr=jnp.pad(moe_router[j], ((0, 0), (0, LANE - N_EXPERTS))),
                       w1=moe_w1[j].astype(bf16), w3=moe_w3[j].astype(bf16), w2=moe_w2[j].astype(bf16))
        layers.append(lay)
    return dict(layers=layers, meta=meta_tokens, gf=final_g.reshape(1, -1),
                seg512=_seg_matrix(D_ATTN, ATT_HEAD_DIM), seg256=_seg_matrix(D_MLSTM, ML_HEAD_DIM))


def _trunk(x, P):
    B, N, D = x.shape
    Lp = N + HEAD_ROWS
    tm = _row_tile(Lp)
    h = jnp.concatenate([jnp.zeros((B, PAD_FRONT, D), x.dtype),
                         jnp.broadcast_to(P['meta'].astype(x.dtype), (B, N_META, D)), x], axis=1)
    cos, sa, sb = _rope_tables(N)
    kbias = jnp.where(jnp.arange(Lp) >= PAD_FRONT, 0.0, NEG).astype(f32).reshape(1, Lp)
    depth = len(P['layers'])
    for l, lay in enumerate(P['layers']):
        u_t, mqk, mv, mo, gates, aq, ak, av = _in_proj(
            h, lay['g1'], lay['w_cat'], cos, sa, sb, lay['qg'], lay['kg'], P['seg512'], tm)
        ys = _s5_scan(u_t, lay['bmat'], lay['cmat'], lay['a_re'], lay['a_im'], B)
        hf, hb = _mlstm(mqk, mv, gates, lay['conv_w'], lay['conv_b'], lay['gate_bias'])
        att = _attention(aq, ak, av, kbias)
        h, xn, gate = _out_proj(h, ys, u_t, hf, hb, mo, att, lay['dsk'], lay['wglu'], lay['mlg'],
                                P['seg256'], lay['wout'], lay['g2'], lay['wr'], tm)
        final = l == depth - 1
        T = B * Lp
        h2, x2 = h.reshape(T, D), xn.reshape(T, D)
        tt = _row_tile(T)
        if lay['moe']:
            h2 = _moe(x2, h2, gate.reshape(T, LANE), lay['w1'], lay['w3'], lay['w2'], P['gf'], tt, final)
        else:
            h2 = _ffn(x2, h2, lay['w1'], lay['w3'], lay['w2'], P['gf'], tt, final)
        h = h2.reshape(B, Lp, D)
    return h[:, HEAD_ROWS:]


def kernel(x_prompt, x_sample, meta_tokens, norm1_g, w_in, w_out, ssm_lam_re, ssm_lam_im, ssm_log_step, ssm_b_re, ssm_b_im, ssm_c_re, ssm_c_im, ssm_d, ssm_w_glu, ml_conv_w, ml_conv_b, ml_b_i, ml_b_f, ml_norm_g, att_q_g, att_k_g, norm2_g, ffn_w1, ffn_w3, ffn_w2, moe_router, moe_w1, moe_w3, moe_w2, final_g):
    P = _prep(meta_tokens, norm1_g, w_in, w_out,
              ssm_lam_re, ssm_lam_im, ssm_log_step, ssm_b_re, ssm_b_im, ssm_c_re, ssm_c_im, ssm_d, ssm_w_glu,
              ml_conv_w, ml_conv_b, ml_b_i, ml_b_f, ml_norm_g, att_q_g, att_k_g,
              norm2_g, ffn_w1, ffn_w3, ffn_w2, moe_router, moe_w1, moe_w3, moe_w2, final_g)
    return (_trunk(x_prompt, P), _trunk(x_sample, P))
```

```python
import functools
import math

import jax
import jax.numpy as jnp
from jax import lax
from jax.experimental import pallas as pl
from jax.experimental.pallas import tpu as pltpu

f32 = jnp.float32
bf16 = jnp.bfloat16

D_MODEL = 1024
N_META = 16
GRID_W = 64
EPS = 1e-6
D_SSM = 256
D_MLSTM = 256
D_ATTN = 512
SSM_GROUP = 16
SSM_GROUPS = 16
SSM_STATE = 64
SSM_LANES = SSM_GROUPS * SSM_STATE
ML_HEADS = 4
ML_HEAD_DIM = 64
ATT_HEADS = 8
ATT_KV_HEADS = 2
ATT_HEAD_DIM = 64
ATT_KV_W = 128
ROPE_AXIS = 32
ROPE_THETA = 10000.0
D_FF = 2816
N_EXPERTS = 8

LANE = 128
HEAD_ROWS = 128
PAD_FRONT = HEAD_ROWS - N_META
ML_CHUNK = 128
S5_CHUNK = 64
ATT_TQ = 128
FF_HALF = D_FF // 2
MOE_TG = 512
MOE_TROW = 256
SLAB = 8
NEG = -1e30
VMEM_LIMIT = 48 * 1024 * 1024

C_U = 0
C_MQK = 256
C_MV = 768
C_MO = 1024
C_G = 1280
C_AQ = 1408
C_AK = 1920
C_AV = 2048
D_INP = 2176


def _dot(a, b):
    return jnp.dot(a, b, preferred_element_type=f32)


def _dot_t(a, b):
    return lax.dot_general(a, b, (((1,), (1,)), ((), ())), preferred_element_type=f32)


def _split2_dot(x, m):
    hi = x.astype(bf16)
    lo = (x - hi.astype(f32)).astype(bf16)
    return _dot(hi, m) + _dot(lo, m)


def _split3(x):
    hi = x.astype(bf16)
    r = x - hi.astype(f32)
    mid = r.astype(bf16)
    lo = (r - mid.astype(f32)).astype(bf16)
    return hi, mid, lo


def _sigmoid(x):
    return 1.0 / (1.0 + jnp.exp(-x))


def _rope(x, cos, sin_a, sin_b):
    w = x.shape[-1]
    xl = pltpu.roll(x, w - 16, 1)
    xr = pltpu.roll(x, 16, 1)
    return x * cos + xl * sin_a + xr * sin_b


def _in_proj_kernel(h_ref, g_ref, w_ref, cos_ref, sa_ref, sb_ref, qg_ref, kg_ref, seg_ref,
                    u_ref, mqk_ref, mv_ref, mo_ref, gt_ref, aq_ref, ak_ref, av_ref):
    h = h_ref[...]
    xn = h * lax.rsqrt(jnp.mean(h * h, axis=-1, keepdims=True) + EPS) * g_ref[...]
    p = _dot(xn.astype(bf16), w_ref[...])
    u_ref[...] = p[:, C_U:C_MQK]
    mqk_ref[...] = p[:, C_MQK:C_MV]
    mv_ref[...] = p[:, C_MV:C_MO].astype(bf16)
    mo_ref[...] = p[:, C_MO:C_G].astype(bf16)
    gt_ref[...] = p[:, C_G:C_AQ]
    cos = cos_ref[...]
    sa = sa_ref[...]
    sb = sb_ref[...]
    seg = seg_ref[...]
    q = p[:, C_AQ:C_AK]
    qn = q * lax.rsqrt(_split2_dot(q * q, seg) + EPS) * qg_ref[...]
    rep = lambda t: jnp.concatenate([t] * (D_ATTN // LANE), axis=1)
    aq_ref[...] = (_rope(qn, rep(cos), rep(sa), rep(sb)) * (ATT_HEAD_DIM ** -0.5)).astype(bf16)
    k = p[:, C_AK:C_AV]
    kn = k * lax.rsqrt(_split2_dot(k * k, seg[:ATT_KV_W, :ATT_KV_W]) + EPS) * kg_ref[...]
    ak_ref[...] = _rope(kn, cos, sa, sb).astype(bf16)
    av_ref[...] = p[:, C_AV:D_INP].astype(bf16)


def _in_proj(h, g, w, cos, sa, sb, qg, kg, seg, tm):
    B, Lp, D = h.shape
    nt = Lp // tm
    row = lambda w_: pl.BlockSpec((None, tm, w_), lambda b, i: (b, i, 0))
    full = lambda a: pl.BlockSpec(a.shape, lambda b, i: (0,) * a.ndim)
    tab = pl.BlockSpec((tm, LANE), lambda b, i: (i, 0))
    return pl.pallas_call(
        _in_proj_kernel,
        grid=(B, nt),
        in_specs=[row(D), full(g), full(w), tab, tab, tab, full(qg), full(kg), full(seg)],
        out_specs=[
            pl.BlockSpec((tm, D_SSM), lambda b, i: (i, b)),
            row(2 * D_MLSTM), row(D_MLSTM), row(D_MLSTM), row(LANE),
            row(D_ATTN), row(ATT_KV_W), row(ATT_KV_W),
        ],
        out_shape=[
            jax.ShapeDtypeStruct((Lp, B * D_SSM), f32),
            jax.ShapeDtypeStruct((B, Lp, 2 * D_MLSTM), f32),
            jax.ShapeDtypeStruct((B, Lp, D_MLSTM), bf16),
            jax.ShapeDtypeStruct((B, Lp, D_MLSTM), bf16),
            jax.ShapeDtypeStruct((B, Lp, LANE), f32),
            jax.ShapeDtypeStruct((B, Lp, D_ATTN), bf16),
            jax.ShapeDtypeStruct((B, Lp, ATT_KV_W), bf16),
            jax.ShapeDtypeStruct((B, Lp, ATT_KV_W), bf16),
        ],
        compiler_params=pltpu.CompilerParams(
            dimension_semantics=("parallel", "arbitrary"), vmem_limit_bytes=VMEM_LIMIT),
        name="in_proj",
    )(h, g, w, cos, sa, sb, qg, kg, seg)


def _s5_kernel(u_ref, bm_ref, cm_ref, are_ref, aim_ref, y_ref, x_ref, st_ref, *, tc, nb):
    d = pl.program_id(0)
    c = pl.program_id(1)

    @pl.when(c == 0)
    def _():
        st_ref[...] = jnp.zeros_like(st_ref)

    u = u_ref[...].reshape(tc * nb, D_SSM).astype(bf16)
    x_ref[...] = _dot(u, bm_ref[...])
    lw = 512
    for lb in range(SSM_LANES // lw):
        re_sl = pl.ds(lb * lw, lw)
        im_sl = pl.ds(SSM_LANES + lb * lw, lw)
        a_re = jnp.broadcast_to(are_ref[:, lb * lw:(lb + 1) * lw], (nb, lw))
        a_im = jnp.broadcast_to(aim_ref[:, lb * lw:(lb + 1) * lw], (nb, lw))

        def step(t, carry):
            xr, xi = carry
            tt = d * (tc - 1 - t) + (1 - d) * t
            r = pl.multiple_of(tt * nb, 8)
            nr = a_re * xr - a_im * xi + x_ref[pl.ds(r, nb), re_sl]
            ni = a_re * xi + a_im * xr + x_ref[pl.ds(r, nb), im_sl]
            x_ref[pl.ds(r, nb), re_sl] = nr
            x_ref[pl.ds(r, nb), im_sl] = ni
            return nr, ni

        xr, xi = lax.fori_loop(0, tc, step, (st_ref[:, re_sl], st_ref[:, im_sl]), unroll=4)
        st_ref[:, re_sl] = xr
        st_ref[:, im_sl] = xi
    y = _dot(x_ref[...].astype(bf16), cm_ref[...])
    y_ref[...] = y.reshape(tc, nb, D_SSM)


def _s5_scan(u_t, bmat, cmat, a_re, a_im, nb):
    Lp = u_t.shape[0]
    tc = S5_CHUNK
    nc = Lp // tc
    cidx = lambda d, c: d * (nc - 1 - c) + (1 - d) * c
    u3 = u_t.reshape(Lp, nb, D_SSM)
    return pl.pallas_call(
        functools.partial(_s5_kernel, tc=tc, nb=nb),
        grid=(2, nc),
        in_specs=[
            pl.BlockSpec((tc, nb, D_SSM), lambda d, c: (cidx(d, c), 0, 0)),
            pl.BlockSpec((None, D_SSM, 2 * SSM_LANES), lambda d, c: (d, 0, 0)),
            pl.BlockSpec((None, 2 * SSM_LANES, D_SSM), lambda d, c: (d, 0, 0)),
            pl.BlockSpec((None, 1, SSM_LANES), lambda d, c: (d, 0, 0)),
            pl.BlockSpec((None, 1, SSM_LANES), lambda d, c: (d, 0, 0)),
        ],
        out_specs=pl.BlockSpec((None, tc, nb, D_SSM), lambda d, c: (d, cidx(d, c), 0, 0)),
        out_shape=jax.ShapeDtypeStruct((2, Lp, nb, D_SSM), f32),
        scratch_shapes=[pltpu.VMEM((tc * nb, 2 * SSM_LANES), f32),
                        pltpu.VMEM((nb, 2 * SSM_LANES), f32)],
        compiler_params=pltpu.CompilerParams(
            dimension_semantics=("arbitrary", "arbitrary"), vmem_limit_bytes=VMEM_LIMIT),
        name="s5_scan",
    )(u3, bmat, cmat, a_re, a_im)


def _conv3(x, prev8, next8, w_ref, b_ref, first, last):
    n = x.shape[0]
    rows = lax.broadcasted_iota(jnp.int32, x.shape, 0)
    pv = jnp.where(first, 0.0, prev8[7:8, :])
    nx = jnp.where(last, 0.0, next8[0:1, :])
    xm = jnp.where(rows == 0, pv, pltpu.roll(x, 1, 0))
    xp = jnp.where(rows == n - 1, nx, pltpu.roll(x, n - 1, 0))
    return xm * w_ref[0:1, :] + x * w_ref[1:2, :] + xp * w_ref[2:3, :] + b_ref[...]


def _mlstm_gates(g, chunk, bias, *, reverse):
    lc = ML_CHUNK
    r_i = lax.broadcasted_iota(jnp.int32, (lc, lc), 0)
    c_i = lax.broadcasted_iota(jnp.int32, (lc, lc), 1)
    tri = ((c_i >= r_i) if reverse else (c_i <= r_i)).astype(bf16)
    gb = g + bias
    pos = chunk * lc + lax.broadcasted_iota(jnp.int32, (lc, LANE), 0)
    valid = pos >= PAD_FRONT
    li = jnp.where(valid, gb, NEG)
    lf = jnp.where(valid, jnp.minimum(gb, 0.0) - jnp.log(1.0 + jnp.exp(-jnp.abs(gb))), 0.0)
    hi, mid, lo = _split3(lf)
    bcol = _dot(tri, hi) + _dot(tri, mid) + _dot(tri, lo)
    return li, li.T, bcol, bcol.T


def _bmm(a, b):
    return lax.dot_general(a, b, (((2,), (1,)), ((0,), (0,))), preferred_element_type=f32)


def _bmm_nt(a, b):
    return lax.dot_general(a, b, (((2,), (2,)), ((0,), (0,))), preferred_element_type=f32)


def _bmm_tn(a, b):
    return lax.dot_general(a, b, (((1,), (1,)), ((0,), (0,))), preferred_element_type=f32)


def _mlstm_kernel(qkf_ref, qkfp_ref, qkfn_ref, vf_ref, gf_ref,
                  qkb_ref, qkbp_ref, qkbn_ref, vb_ref, gb_ref,
                  cw_ref, cb_ref, bias_ref, hf_ref, hb_ref, ct_ref, n_ref, m_ref, *, nc):
    c = pl.program_id(1)
    lc = ML_CHUNK
    H = ML_HEADS
    nb = 2 * H

    @pl.when(c == 0)
    def _():
        ct_ref[...] = jnp.zeros_like(ct_ref)
        n_ref[...] = jnp.zeros_like(n_ref)
        m_ref[...] = jnp.zeros_like(m_ref)

    cb = nc - 1 - c
    bias = bias_ref[...]
    qk_f = _conv3(qkf_ref[...], qkfp_ref[...], qkfn_ref[...], cw_ref, cb_ref, c == 0, c == nc - 1)
    qk_b = _conv3(qkb_ref[...], qkbp_ref[...], qkbn_ref[...], cw_ref, cb_ref, cb == 0, cb == nc - 1)
    qs, ks, vs, bcs, brs, lirs, lics, bends = [], [], [], [], [], [], [], []
    for d, (qk, v, g, chunk) in enumerate(((qk_f, vf_ref[...], gf_ref[...], c),
                                           (qk_b, vb_ref[...], gb_ref[...], cb))):
        li, li_t, bcol, b_t = _mlstm_gates(g, chunk, bias, reverse=bool(d))
        q_all = qk[:, :D_MLSTM].astype(bf16)
        k_all = (qk[:, D_MLSTM:] * (ML_HEAD_DIM ** -0.5)).astype(bf16)
        e_row = 0 if d else lc - 1
        for hh in range(H):
            ci = d * H + hh
            cf = nb + ci
            sl = slice(hh * ML_HEAD_DIM, (hh + 1) * ML_HEAD_DIM)
            qs.append(q_all[:, sl])
            ks.append(k_all[:, sl])
            vs.append(v[:, sl])
            bcs.append(bcol[:, cf:cf + 1])
            brs.append(b_t[cf:cf + 1, :])
            lirs.append(li_t[ci:ci + 1, :])
            lics.append(li[:, ci:ci + 1])
            bends.append(b_t[cf:cf + 1, e_row:e_row + 1])
    q, k, v = jnp.stack(qs), jnp.stack(ks), jnp.stack(vs)
    bc, br = jnp.stack(bcs), jnp.stack(brs)
    lir, lic, b_end = jnp.stack(lirs), jnp.stack(lics), jnp.stack(bends)
    ct, n, m_prev = ct_ref[...], n_ref[...], m_ref[...][:, :, 0:1]
    bi = lax.broadcasted_iota(jnp.int32, (nb, lc, lc), 0)
    r_i = lax.broadcasted_iota(jnp.int32, (nb, lc, lc), 1)
    c_i = lax.broadcasted_iota(jnp.int32, (nb, lc, lc), 2)
    keep = jnp.where(bi < H, c_i - r_i, r_i - c_i) <= 0
    dm = jnp.where(keep, bc - br + lir, NEG)
    a = bc + m_prev
    m_t = jnp.maximum(a, jnp.max(dm, axis=2, keepdims=True))
    s = _bmm_nt(q, k) * jnp.exp(dm - m_t)
    e = jnp.exp(a - m_t)
    num = _bmm(s.astype(bf16), v) + e * _bmm(q, ct.astype(bf16))
    den = jnp.sum(s, axis=2, keepdims=True) + e * jnp.sum(q.astype(f32) * n, axis=2, keepdims=True)
    h = num / jnp.maximum(jnp.abs(den), jnp.exp(-m_t))
    hf_ref[...] = jnp.concatenate([h[i] for i in range(H)], axis=1)
    hb_ref[...] = jnp.concatenate([h[H + i] for i in range(H)], axis=1)
    m_loc = jnp.max(b_end - br + lir, axis=2, keepdims=True)
    kw = k.astype(f32) * jnp.exp(b_end - bc + lic - m_loc)
    ct_loc = _bmm_tn(kw.astype(bf16), v)
    n_loc = jnp.sum(kw, axis=1, keepdims=True)
    m_new = jnp.maximum(b_end + m_prev, m_loc)
    s_old = jnp.exp(b_end + m_prev - m_new)
    s_loc = jnp.exp(m_loc - m_new)
    ct_ref[...] = s_old * ct + s_loc * ct_loc
    n_ref[...] = s_old * n + s_loc * n_loc
    m_ref[...] = jnp.broadcast_to(m_new, (nb, 1, LANE))


def _mlstm(mqk, mv, gates, conv_w, conv_b, bias):
    B, Lp, _ = mqk.shape
    lc = ML_CHUNK
    nc = Lp // lc
    n8 = Lp // 8
    per = lc // 8
    fwd = lambda b, c: (b, c, 0)
    bwd = lambda b, c: (b, nc - 1 - c, 0)
    prev = lambda f: (lambda b, c: (b, jnp.maximum(f(b, c)[1] * per - 1, 0), 0))
    nxt = lambda f: (lambda b, c: (b, jnp.minimum((f(b, c)[1] + 1) * per, n8 - 1), 0))
    full = lambda a: pl.BlockSpec(a.shape, lambda b, c: (0,) * a.ndim)
    w2 = 2 * D_MLSTM

    def specs(f):
        return [pl.BlockSpec((None, lc, w2), f), pl.BlockSpec((None, 8, w2), prev(f)),
                pl.BlockSpec((None, 8, w2), nxt(f)), pl.BlockSpec((None, lc, D_MLSTM), f),
                pl.BlockSpec((None, lc, LANE), f)]

    return pl.pallas_call(
        functools.partial(_mlstm_kernel, nc=nc),
        grid=(B, nc),
        in_specs=specs(fwd) + specs(bwd) + [full(conv_w), full(conv_b), full(bias)],
        out_specs=[pl.BlockSpec((None, lc, D_MLSTM), fwd), pl.BlockSpec((None, lc, D_MLSTM), bwd)],
        out_shape=[jax.ShapeDtypeStruct((B, Lp, D_MLSTM), f32)] * 2,
        scratch_shapes=[pltpu.VMEM((2 * ML_HEADS, ML_HEAD_DIM, ML_HEAD_DIM), f32),
                        pltpu.VMEM((2 * ML_HEADS, 1, ML_HEAD_DIM), f32),
                        pltpu.VMEM((2 * ML_HEADS, 1, LANE), f32)],
        compiler_params=pltpu.CompilerParams(
            dimension_semantics=("parallel", "arbitrary"), vmem_limit_bytes=VMEM_LIMIT),
        name="mlstm",
    )(mqk, mqk, mqk, mv, gates, mqk, mqk, mqk, mv, gates, conv_w, conv_b, bias)


def _attn_kernel(q_ref, k_ref, v_ref, kb_ref, o_ref):
    tq = q_ref.shape[0]
    grp = ATT_HEADS // ATT_KV_HEADS
    q = q_ref[...]
    kb = kb_ref[...]
    outs = []
    for g in range(ATT_KV_HEADS):
        kh = k_ref[:, g * ATT_HEAD_DIM:(g + 1) * ATT_HEAD_DIM]
        vh = v_ref[:, g * ATT_HEAD_DIM:(g + 1) * ATT_HEAD_DIM]
        q4 = jnp.concatenate(
            [q[:, (g * grp + j) * ATT_HEAD_DIM:(g * grp + j + 1) * ATT_HEAD_DIM] for j in range(grp)], axis=0)
        s = _dot_t(q4, kh) + kb
        m = jnp.max(s, axis=1, keepdims=True)
        p = jnp.exp(s - m)
        l = jnp.sum(p, axis=1, keepdims=True)
        o = _dot(p.astype(bf16), vh) / l
        outs.extend(o[j * tq:(j + 1) * tq] for j in range(grp))
    o_ref[...] = jnp.concatenate(outs, axis=1).astype(bf16)


def _attention(aq, ak, av, kbias):
    B, Lp, _ = aq.shape
    tq = ATT_TQ
    return pl.pallas_call(
        _attn_kernel,
        grid=(B, Lp // tq),
        in_specs=[pl.BlockSpec((None, tq, D_ATTN), lambda b, i: (b, i, 0)),
                  pl.BlockSpec((None, Lp, ATT_KV_W), lambda b, i: (b, 0, 0)),
                  pl.BlockSpec((None, Lp, ATT_KV_W), lambda b, i: (b, 0, 0)),
                  pl.BlockSpec((1, Lp), lambda b, i: (0, 0))],
        out_specs=pl.BlockSpec((None, tq, D_ATTN), lambda b, i: (b, i, 0)),
        out_shape=jax.ShapeDtypeStruct((B, Lp, D_ATTN), bf16),
        compiler_params=pltpu.CompilerParams(
            dimension_semantics=("parallel", "arbitrary"), vmem_limit_bytes=VMEM_LIMIT),
        name="attention",
    )(aq, ak, av, kbias)


def _out_proj_kernel(h_ref, ys_ref, u_ref, hf_ref, hb_ref, mo_ref, att_ref,
                     dsk_ref, wglu_ref, mlg_ref, seg_ref, wout_ref, g2_ref, wr_ref,
                     hn_ref, xn_ref, gate_ref, *, tm):
    i = pl.program_id(1)
    u = u_ref[...]
    y = ys_ref[0] + ys_ref[1] + dsk_ref[...] * u
    y = 0.5 * y * (1.0 + jnp.tanh(math.sqrt(2.0 / math.pi) * (y + 0.044715 * (y * y * y))))
    ag = _dot(y.astype(bf16), wglu_ref[...])
    y_ssm = ag[:, :D_SSM] * _sigmoid(ag[:, D_SSM:])
    hm = hf_ref[...] + hb_ref[...]
    hn = hm * lax.rsqrt(_split2_dot(hm * hm, seg_ref[...]) + EPS) * mlg_ref[...]
    y_ml = _sigmoid(mo_ref[...].astype(f32)) * hn
    acc = (_dot(y_ssm.astype(bf16), wout_ref[0:D_SSM, :])
           + _dot(y_ml.astype(bf16), wout_ref[D_SSM:D_SSM + D_MLSTM, :])
           + _dot(att_ref[...], wout_ref[D_SSM + D_MLSTM:, :]))
    pos = i * tm + lax.broadcasted_iota(jnp.int32, (tm, 1), 0)
    h_new = jnp.where(pos >= PAD_FRONT, h_ref[...] + acc, 0.0)
    hn_ref[...] = h_new
    xn = h_new * lax.rsqrt(jnp.mean(h_new * h_new, axis=-1, keepdims=True) + EPS) * g2_ref[...]
    xn_ref[...] = xn.astype(bf16)
    logits = jnp.dot(xn, wr_ref[...], preferred_element_type=f32, precision=lax.Precision.HIGHEST)
    lane = lax.broadcasted_iota(jnp.int32, logits.shape, 1).astype(f32)
    lg = jnp.where(lane < N_EXPERTS, logits, NEG)
    v1 = jnp.max(lg, axis=1, keepdims=True)
    i1 = jnp.min(jnp.where(lg == v1, lane, float(LANE)), axis=1, keepdims=True)
    lg2 = jnp.where(lane == i1, NEG, lg)
    v2 = jnp.max(lg2, axis=1, keepdims=True)
    i2 = jnp.min(jnp.where(lg2 == v2, lane, float(LANE)), axis=1, keepdims=True)
    g1 = 1.0 / (1.0 + jnp.exp(v2 - v1))
    gate_ref[...] = jnp.where(lane == 0.0, i1, jnp.where(lane == 1.0, i2, jnp.where(
        lane == 2.0, g1, jnp.where(lane == 3.0, 1.0 - g1, 0.0))))


def _out_proj(h, ys, u_t, hf, hb, mo, att, dsk, wglu, mlg, seg, wout, g2, wr, tm):
    B, Lp, D = h.shape
    row = lambda w_: pl.BlockSpec((None, tm, w_), lambda b, i: (b, i, 0))
    full = lambda a: pl.BlockSpec(a.shape, lambda b, i: (0,) * a.ndim)
    ys2 = ys.reshape(2, Lp, B * D_SSM)
    return pl.pallas_call(
        functools.partial(_out_proj_kernel, tm=tm),
        grid=(B, Lp // tm),
        in_specs=[row(D),
                  pl.BlockSpec((2, tm, D_SSM), lambda b, i: (0, i, b)),
                  pl.BlockSpec((tm, D_SSM), lambda b, i: (i, b)),
                  row(D_MLSTM), row(D_MLSTM), row(D_MLSTM), row(D_ATTN),
                  full(dsk), full(wglu), full(mlg), full(seg), full(wout), full(g2), full(wr)],
        out_specs=[row(D), row(D), row(LANE)],
        out_shape=[jax.ShapeDtypeStruct((B, Lp, D), f32),
                   jax.ShapeDtypeStruct((B, Lp, D), bf16),
                   jax.ShapeDtypeStruct((B, Lp, LANE), f32)],
        compiler_params=pltpu.CompilerParams(
            dimension_semantics=("parallel", "arbitrary"), vmem_limit_bytes=VMEM_LIMIT),
        name="out_proj",
    )(h, ys2, u_t, hf, hb, mo, att, dsk, wglu, mlg, seg, wout, g2, wr)


def _swiglu_part(x, w1, w3, w2):
    a = _dot(x, w1)
    b = _dot(x, w3)
    return _dot((a * _sigmoid(a) * b).astype(bf16), w2)


def _final_norm(h, gf):
    return h * lax.rsqrt(jnp.mean(h * h, axis=-1, keepdims=True) + EPS) * gf


def _ffn_kernel(x_ref, h_ref, w1_ref, w3_ref, w2_ref, gf_ref, o_ref, *, final):
    j = pl.program_id(1)
    part = _swiglu_part(x_ref[...], w1_ref[...], w3_ref[...], w2_ref[...])

    @pl.when(j == 0)
    def _():
        o_ref[...] = h_ref[...] + part

    @pl.when(j > 0)
    def _():
        o_ref[...] += part

    if final:
        @pl.when(j == pl.num_programs(1) - 1)
        def _():
            o_ref[...] = _final_norm(o_ref[...], gf_ref[...])


def _ffn(x, h, w1, w3, w2, gf, tm, final):
    T, D = h.shape
    nj = D_FF // FF_HALF
    return pl.pallas_call(
        functools.partial(_ffn_kernel, final=final),
        grid=(T // tm, nj),
        in_specs=[pl.BlockSpec((tm, D), lambda i, j: (i, 0)),
                  pl.BlockSpec((tm, D), lambda i, j: (i, 0)),
                  pl.BlockSpec((D, FF_HALF), lambda i, j: (0, j)),
                  pl.BlockSpec((D, FF_HALF), lambda i, j: (0, j)),
                  pl.BlockSpec((FF_HALF, D), lambda i, j: (j, 0)),
                  pl.BlockSpec((1, D), lambda i, j: (0, 0))],
        out_specs=pl.BlockSpec((tm, D), lambda i, j: (i, 0)),
        out_shape=jax.ShapeDtypeStruct((T, D), f32),
        compiler_params=pltpu.CompilerParams(
            dimension_semantics=("parallel", "arbitrary"), vmem_limit_bytes=VMEM_LIMIT),
        name="ffn",
    )(x, h, w1, w3, w2, gf)


def _route_kernel(r_ref, pos_ref, cnt_ref, run_ref, *, tg):
    p = pl.program_id(0)
    i = pl.program_id(1)
    tr = r_ref.shape[0]
    r = r_ref[...]
    lane = lax.broadcasted_iota(jnp.int32, (tr, LANE), 1).astype(f32)
    oh1 = (lane == r[:, 0:1]).astype(f32)
    oh2 = (lane == r[:, 1:2]).astype(f32)
    both = oh1 + oh2
    tile_cnt = jnp.sum(both, axis=0, keepdims=True)

    @pl.when(jnp.logical_and(p == 0, i == 0))
    def _():
        run_ref[...] = jnp.zeros_like(run_ref)

    @pl.when(p == 0)
    def _():
        run_ref[...] += tile_cnt

    @pl.when(jnp.logical_and(p == 1, i == 0))
    def _():
        cnt = run_ref[...]
        cnt_ref[...] = cnt.astype(jnp.int32)
        padded = jnp.broadcast_to(jnp.ceil(cnt * (1.0 / tg)) * tg, (8, LANE))
        e_r = lax.broadcasted_iota(jnp.int32, (LANE, LANE), 0)
        e_c = lax.broadcasted_iota(jnp.int32, (LANE, LANE), 1)
        before = (e_r < e_c).astype(bf16)
        hi, mid, lo = _split3(padded)
        run_ref[...] = (_dot(hi, before) + _dot(mid, before) + _dot(lo, before))[0:1, :]

    @pl.when(p == 1)
    def _():
        t_r = lax.broadcasted_iota(jnp.int32, (tr, tr), 0)
        t_c = lax.broadcasted_iota(jnp.int32, (tr, tr), 1)
        earlier = (t_c < t_r).astype(bf16)
        base = run_ref[...] + _dot(earlier, both.astype(bf16))
        p1 = jnp.sum(oh1 * base, axis=1, keepdims=True)
        p2 = jnp.sum(oh2 * base, axis=1, keepdims=True)
        pos_ref[...] = jnp.where(lane == 0.0, p1, jnp.where(lane == 1.0, p2, 0.0)).astype(jnp.int32)
        run_ref[...] += tile_cnt


def _route(route, tg):
    T = route.shape[0]
    tr = _row_tile(T)
    return pl.pallas_call(
        functools.partial(_route_kernel, tg=tg),
        grid=(2, T // tr),
        in_specs=[pl.BlockSpec((tr, LANE), lambda p, i: (i, 0))],
        out_specs=[pl.BlockSpec((tr, LANE), lambda p, i: (i * p, 0)),
                   pl.BlockSpec((1, LANE), lambda p, i: (0, 0))],
        out_shape=[jax.ShapeDtypeStruct((T, LANE), jnp.int32),
                   jax.ShapeDtypeStruct((1, LANE), jnp.int32)],
        scratch_shapes=[pltpu.VMEM((1, LANE), f32)],
        compiler_params=pltpu.CompilerParams(
            dimension_semantics=("arbitrary", "arbitrary"), vmem_limit_bytes=VMEM_LIMIT),
        name="moe_route",
    )(route)


def _slab_rows(ref, row, n=1):
    return ref.at[pl.ds(pl.multiple_of(row * SLAB, SLAB), n * SLAB)]


def _dispatch_kernel(pos_ref, x_ref, xs_in_ref, xs_ref, buf_ref, sem, *, td, T):
    del xs_in_ref
    base = pl.program_id(0) * td
    xf = x_ref[...].astype(f32)
    for s in range(SLAB):
        buf_ref[pl.ds(s, td, stride=SLAB), :] = xf[:, s * LANE:(s + 1) * LANE]

    def copies(r):
        src = _slab_rows(buf_ref, r)
        return (pltpu.make_async_copy(src, _slab_rows(xs_ref, pos_ref[base + r]), sem.at[0]),
                pltpu.make_async_copy(src, _slab_rows(xs_ref, pos_ref[T + base + r]), sem.at[1]))

    def issue(r, carry):
        for cp in copies(r):
            cp.start()
        return carry

    def drain(r, carry):
        for cp in copies(r):
            cp.wait()
        return carry

    lax.fori_loop(0, td, issue, 0, unroll=8)
    lax.fori_loop(0, td, drain, 0, unroll=8)


def _dispatch(pos_flat, x, n_rows, td):
    T, D = x.shape
    xs0 = jnp.zeros((n_rows * SLAB, LANE), f32)
    return pl.pallas_call(
        functools.partial(_dispatch_kernel, td=td, T=T),
        grid_spec=pltpu.PrefetchScalarGridSpec(
            num_scalar_prefetch=1,
            grid=(T // td,),
            in_specs=[pl.BlockSpec((td, D), lambda i, pos: (i, 0)),
                      pl.BlockSpec(memory_space=pl.ANY)],
            out_specs=pl.BlockSpec(memory_space=pl.ANY),
            scratch_shapes=[pltpu.VMEM((td * SLAB, LANE), f32), pltpu.SemaphoreType.DMA((2,))]),
        out_shape=jax.ShapeDtypeStruct((n_rows * SLAB, LANE), f32),
        input_output_aliases={2: 0},
        compiler_params=pltpu.CompilerParams(
            dimension_semantics=("arbitrary",), vmem_limit_bytes=VMEM_LIMIT),
        name="moe_dispatch",
    )(pos_flat, x, xs0)


def _gmm_kernel(te_ref, nu_ref, xs_ref, w1_ref, w3_ref, w2_ref, ys_ref, acc_ref, *, tg):
    j = pl.program_id(0)
    f = pl.program_id(1)
    nf = pl.num_programs(1)

    @pl.when(j < nu_ref[0])
    def _():
        x = jnp.concatenate([xs_ref[pl.ds(s, tg, stride=SLAB), :] for s in range(SLAB)], axis=1)
        part = _swiglu_part(x.astype(bf16), w1_ref[...], w3_ref[...], w2_ref[...])

        @pl.when(f == 0)
        def _():
            acc_ref[...] = part

        @pl.when(jnp.logical_and(f > 0, f < nf - 1))
        def _():
            acc_ref[...] += part

        @pl.when(f == nf - 1)
        def _():
            tot = acc_ref[...] + part
            for s in range(SLAB):
                ys_ref[pl.ds(s, tg, stride=SLAB), :] = tot[:, s * LANE:(s + 1) * LANE]

    @pl.when(jnp.logical_and(j >= nu_ref[0], f == nf - 1))
    def _():
        ys_ref[...] = jnp.zeros_like(ys_ref)


def _gmm(tile_expert, n_used, xs, w1, w3, w2, tg):
    n_tiles = xs.shape[0] // (tg * SLAB)
    nf = D_FF // FF_HALF
    tile = lambda j, f, te, nu: (jnp.minimum(j, nu[0] - 1), 0)
    return pl.pallas_call(
        functools.partial(_gmm_kernel, tg=tg),
        grid_spec=pltpu.PrefetchScalarGridSpec(
            num_scalar_prefetch=2,
            grid=(n_tiles, nf),
            in_specs=[pl.BlockSpec((tg * SLAB, LANE), tile),
                      pl.BlockSpec((None, D_MODEL, FF_HALF), lambda j, f, te, nu: (te[j], 0, f)),
                      pl.BlockSpec((None, D_MODEL, FF_HALF), lambda j, f, te, nu: (te[j], 0, f)),
                      pl.BlockSpec((None, FF_HALF, D_MODEL), lambda j, f, te, nu: (te[j], f, 0))],
            out_specs=pl.BlockSpec((tg * SLAB, LANE), lambda j, f, te, nu: (j, 0)),
            scratch_shapes=[pltpu.VMEM((tg, D_MODEL), f32)]),
        out_shape=jax.ShapeDtypeStruct(xs.shape, f32),
        compiler_params=pltpu.CompilerParams(
            dimension_semantics=("arbitrary", "arbitrary"), vmem_limit_bytes=VMEM_LIMIT),
        name="moe_gmm",
    )(tile_expert, n_used, xs, w1, w3, w2)


def _combine_kernel(pos_ref, h_ref, r_ref, ys_ref, gf_ref, o_ref, b1_ref, b2_ref, sem, *, tc, T, final):
    base = pl.program_id(0) * tc

    def copies(r):
        return (pltpu.make_async_copy(_slab_rows(ys_ref, pos_ref[base + r]), _slab_rows(b1_ref, r), sem.at[0]),
                pltpu.make_async_copy(_slab_rows(ys_ref, pos_ref[T + base + r]), _slab_rows(b2_ref, r), sem.at[1]))

    def issue(r, carry):
        for cp in copies(r):
            cp.start()
        return carry

    def drain(r, carry):
        for cp in copies(r):
            cp.wait()
        return carry

    lax.fori_loop(0, tc, issue, 0, unroll=8)
    lax.fori_loop(0, tc, drain, 0, unroll=8)
    rows = lambda b: jnp.concatenate([b[pl.ds(s, tc, stride=SLAB), :] for s in range(SLAB)], axis=1)
    r = r_ref[...]
    out = h_ref[...] + r[:, 2:3] * rows(b1_ref) + r[:, 3:4] * rows(b2_ref)
    o_ref[...] = _final_norm(out, gf_ref[...]) if final else out


def _combine(pos_flat, h, route, ys, gf, tc, final):
    T, D = h.shape
    return pl.pallas_call(
        functools.partial(_combine_kernel, tc=tc, T=T, final=final),
        grid_spec=pltpu.PrefetchScalarGridSpec(
            num_scalar_prefetch=1,
            grid=(T // tc,),
            in_specs=[pl.BlockSpec((tc, D), lambda i, pos: (i, 0)),
                      pl.BlockSpec((tc, LANE), lambda i, pos: (i, 0)),
                      pl.BlockSpec(memory_space=pl.ANY),
                      pl.BlockSpec((1, D), lambda i, pos: (0, 0))],
            out_specs=pl.BlockSpec((tc, D), lambda i, pos: (i, 0)),
            scratch_shapes=[pltpu.VMEM((tc * SLAB, LANE), f32), pltpu.VMEM((tc * SLAB, LANE), f32),
                            pltpu.SemaphoreType.DMA((2,))]),
        out_shape=jax.ShapeDtypeStruct((T, D), f32),
        compiler_params=pltpu.CompilerParams(
            dimension_semantics=("arbitrary",), vmem_limit_bytes=VMEM_LIMIT),
        name="moe_combine",
    )(pos_flat, h, route, ys, gf)


def _moe(x, h, route, w1, w3, w2, gf, final):
    T, D = h.shape
    tg = MOE_TG
    n_tiles = -(-(2 * T + N_EXPERTS * (tg - 1)) // tg)
    pos, cnt = _route(route, tg)
    ends = jnp.cumsum(-(-cnt[0, :N_EXPERTS] // tg) * tg)
    n_used = (ends[-1:] // tg).astype(jnp.int32)
    tile_expert = jnp.minimum(
        jnp.sum(ends[None, :] <= (jnp.arange(n_tiles, dtype=jnp.int32) * tg)[:, None], axis=1),
        N_EXPERTS - 1).astype(jnp.int32)
    pos_flat = jnp.concatenate([pos[:, 0], pos[:, 1]])
    xs = _dispatch(pos_flat, x, n_tiles * tg, MOE_TROW)
    ys = _gmm(tile_expert, n_used, xs, w1, w3, w2, tg)
    return _combine(pos_flat, h, route, ys, gf, MOE_TROW, final)


def _s5_discretise(lam_re, lam_im, log_step, b_re, b_im, c_re, c_im):
    G, P, C = SSM_GROUPS, SSM_STATE, SSM_GROUP
    dt = jnp.exp(log_step)[..., None]
    mag = jnp.exp(lam_re * dt)
    ab_re = mag * jnp.cos(lam_im * dt)
    ab_im = mag * jnp.sin(lam_im * dt)
    den = lam_re * lam_re + lam_im * lam_im
    nr = ab_re - 1.0
    ni = ab_im
    coef_re = (nr * lam_re + ni * lam_im) / den
    coef_im = (ni * lam_re - nr * lam_im) / den
    bb_re = coef_re[..., None] * b_re - coef_im[..., None] * b_im
    bb_im = coef_re[..., None] * b_im + coef_im[..., None] * b_re
    eye = jnp.eye(G, dtype=f32)

    def bdiag_in(bb):
        return jnp.einsum('dgpc,gh->dgchp', bb, eye).reshape(2, G * C, G * P)

    def bdiag_out(cc):
        return jnp.einsum('dgcp,gh->dgphc', cc, eye).reshape(2, G * P, G * C)

    bmat = jnp.concatenate([bdiag_in(bb_re), bdiag_in(bb_im)], axis=2).astype(bf16)
    cmat = jnp.concatenate([bdiag_out(c_re), -bdiag_out(c_im)], axis=1).astype(bf16)
    return bmat, cmat, ab_re.reshape(2, 1, G * P), ab_im.reshape(2, 1, G * P)


def _rope_tables(n_tokens):
    rows = n_tokens // GRID_W
    row = jnp.concatenate([jnp.zeros((PAD_FRONT,), f32), jnp.full((N_META,), -1.0, f32),
                           jnp.repeat(jnp.arange(rows, dtype=f32), GRID_W)])
    col = jnp.concatenate([jnp.zeros((PAD_FRONT,), f32), jnp.arange(N_META, dtype=f32),
                           jnp.tile(jnp.arange(GRID_W, dtype=f32), rows)])
    inv = ROPE_THETA ** (-jnp.arange(0, ROPE_AXIS, 2, dtype=f32) / ROPE_AXIS)
    ar = row[:, None] * inv
    ac = col[:, None] * inv
    ang = jnp.concatenate([ar, ar, ac, ac], -1)
    cos, sin = jnp.cos(ang), jnp.sin(ang)
    first_half = (jnp.arange(ATT_HEAD_DIM) % ROPE_AXIS) < (ROPE_AXIS // 2)
    sa = jnp.where(first_half, -sin, 0.0)
    sb = jnp.where(first_half, 0.0, sin)
    two = lambda t: jnp.concatenate([t, t], axis=1)
    return two(cos), two(sa), two(sb)


def _seg_matrix(width, seg):
    idx = jnp.arange(width) // seg
    return (idx[:, None] == idx[None, :]).astype(f32).astype(bf16) * jnp.asarray(1.0 / seg, bf16)


def _row_tile(lp):
    best = 16
    for t in range(16, lp + 1, 16):
        if lp % t == 0 and abs(t - 512) < abs(best - 512):
            best = t
    return best


def _prep(meta_tokens, norm1_g, w_in, w_out,
          ssm_lam_re, ssm_lam_im, ssm_log_step, ssm_b_re, ssm_b_im, ssm_c_re, ssm_c_im, ssm_d, ssm_w_glu,
          ml_conv_w, ml_conv_b, ml_b_i, ml_b_f, ml_norm_g, att_q_g, att_k_g,
          norm2_g, ffn_w1, ffn_w3, ffn_w2, moe_router, moe_w1, moe_w3, moe_w2, final_g):
    depth = w_in.shape[0]
    layers = []
    for l in range(depth):
        w = w_in[l]
        wg = jnp.pad(w[:, 1280:1296], ((0, 0), (0, LANE - 16)))
        w_cat = jnp.concatenate([w[:, 0:1280], wg, w[:, 1296:]], axis=1).astype(bf16)
        bmat, cmat, a_re, a_im = _s5_discretise(ssm_lam_re[l], ssm_lam_im[l], ssm_log_step[l],
                                                ssm_b_re[l], ssm_b_im[l], ssm_c_re[l], ssm_c_im[l])
        gate_bias = jnp.pad(jnp.concatenate([ml_b_i[l].reshape(-1), ml_b_f[l].reshape(-1)]),
                            (0, LANE - 4 * ML_HEADS)).reshape(1, LANE)
        lay = dict(
            g1=norm1_g[l].reshape(1, -1), w_cat=w_cat,
            qg=jnp.tile(att_q_g[l], ATT_HEADS).reshape(1, -1),
            kg=jnp.tile(att_k_g[l], ATT_KV_HEADS).reshape(1, -1),
            bmat=bmat, cmat=cmat, a_re=a_re, a_im=a_im,
            dsk=ssm_d[l].reshape(1, -1), wglu=ssm_w_glu[l].astype(bf16),
            conv_w=ml_conv_w[l], conv_b=ml_conv_b[l].reshape(1, -1), gate_bias=gate_bias,
            mlg=ml_norm_g[l].reshape(1, -1), wout=w_out[l].astype(bf16),
            g2=norm2_g[l].reshape(1, -1),
        )
        j = l // 2
        if l % 2 == 0:
            lay.update(moe=False, wr=jnp.zeros((D_MODEL, LANE), f32),
                       w1=ffn_w1[j].astype(bf16), w3=ffn_w3[j].astype(bf16), w2=ffn_w2[j].astype(bf16))
        else:
            lay.update(moe=True, wr=jnp.pad(moe_router[j], ((0, 0), (0, LANE - N_EXPERTS))),
                       w1=moe_w1[j].astype(bf16), w3=moe_w3[j].astype(bf16), w2=moe_w2[j].astype(bf16))
        layers.append(lay)
    return dict(layers=layers, meta=meta_tokens, gf=final_g.reshape(1, -1),
                seg512=_seg_matrix(D_ATTN, ATT_HEAD_DIM), seg256=_seg_matrix(D_MLSTM, ML_HEAD_DIM))


def _trunk(x, P):
    B, N, D = x.shape
    Lp = N + HEAD_ROWS
    tm = _row_tile(Lp)
    h = jnp.concatenate([jnp.zeros((B, PAD_FRONT, D), x.dtype),
                         jnp.broadcast_to(P['meta'].astype(x.dtype), (B, N_META, D)), x], axis=1)
    cos, sa, sb = _rope_tables(N)
    kbias = jnp.where(jnp.arange(Lp) >= PAD_FRONT, 0.0, NEG).astype(f32).reshape(1, Lp)
    depth = len(P['layers'])
    for l, lay in enumerate(P['layers']):
        u_t, mqk, mv, mo, gates, aq, ak, av = _in_proj(
            h, lay['g1'], lay['w_cat'], cos, sa, sb, lay['qg'], lay['kg'], P['seg512'], tm)
        ys = _s5_scan(u_t, lay['bmat'], lay['cmat'], lay['a_re'], lay['a_im'], B)
        hf, hb = _mlstm(mqk, mv, gates, lay['conv_w'], lay['conv_b'], lay['gate_bias'])
        att = _attention(aq, ak, av, kbias)
        h, xn, gate = _out_proj(h, ys, u_t, hf, hb, mo, att, lay['dsk'], lay['wglu'], lay['mlg'],
                                P['seg256'], lay['wout'], lay['g2'], lay['wr'], tm)
        final = l == depth - 1
        T = B * Lp
        h2, x2 = h.reshape(T, D), xn.reshape(T, D)
        tt = _row_tile(T)
        if lay['moe']:
            h2 = _moe(x2, h2, gate.reshape(T, LANE), lay['w1'], lay['w3'], lay['w2'], P['gf'], final)
        else:
            h2 = _ffn(x2, h2, lay['w1'], lay['w3'], lay['w2'], P['gf'], tt, final)
        h = h2.reshape(B, Lp, D)
    return h[:, HEAD_ROWS:]


def kernel(x_prompt, x_sample, meta_tokens, norm1_g, w_in, w_out, ssm_lam_re, ssm_lam_im, ssm_log_step, ssm_b_re, ssm_b_im, ssm_c_re, ssm_c_im, ssm_d, ssm_w_glu, ml_conv_w, ml_conv_b, ml_b_i, ml_b_f, ml_norm_g, att_q_g, att_k_g, norm2_g, ffn_w1, ffn_w3, ffn_w2, moe_router, moe_w1, moe_w3, moe_w2, final_g):
    P = _prep(meta_tokens, norm1_g, w_in, w_out,
              ssm_lam_re, ssm_lam_im, ssm_log_step, ssm_b_re, ssm_b_im, ssm_c_re, ssm_c_im, ssm_d, ssm_w_glu,
              ml_conv_w, ml_conv_b, ml_b_i, ml_b_f, ml_norm_g, att_q_g, att_k_g,
              norm2_g, ffn_w1, ffn_w3, ffn_w2, moe_router, moe_w1, moe_w3, moe_w2, final_g)
    return (_trunk(x_prompt, P), _trunk(x_sample, P))
```

```python
import functools
import math

import jax
import jax.numpy as jnp
from jax import lax
from jax.experimental import pallas as pl
from jax.experimental.pallas import tpu as pltpu

f32 = jnp.float32
bf16 = jnp.bfloat16

D_MODEL = 1024
N_META = 16
GRID_W = 64
EPS = 1e-6
D_SSM = 256
D_MLSTM = 256
D_ATTN = 512
SSM_GROUP = 16
SSM_GROUPS = 16
SSM_STATE = 64
SSM_LANES = SSM_GROUPS * SSM_STATE
ML_HEADS = 4
ML_HEAD_DIM = 64
ATT_HEADS = 8
ATT_KV_HEADS = 2
ATT_HEAD_DIM = 64
ATT_KV_W = 128
ROPE_AXIS = 32
ROPE_THETA = 10000.0
D_FF = 2816
N_EXPERTS = 8

LANE = 128
HEAD_ROWS = 128
PAD_FRONT = HEAD_ROWS - N_META
ML_CHUNK = 128
S5_CHUNK = 64
ATT_TQ = 128
LOG2E = 1.4426950408889634
FF_HALF = D_FF // 2
MOE_TG = 512
MOE_TROW = 256
SLAB = 8
NEG = -1e30
VMEM_LIMIT = 48 * 1024 * 1024

C_U = 0
C_MQK = 256
C_MV = 768
C_MO = 1024
C_G = 1280
C_AQ = 1408
C_AK = 1920
C_AV = 2048
D_INP = 2176


def _dot(a, b):
    return jnp.dot(a, b, preferred_element_type=f32)


def _dot_t(a, b):
    return lax.dot_general(a, b, (((1,), (1,)), ((), ())), preferred_element_type=f32)


def _split2_dot(x, m):
    hi = x.astype(bf16)
    lo = (x - hi.astype(f32)).astype(bf16)
    return _dot(hi, m) + _dot(lo, m)


def _split3(x):
    hi = x.astype(bf16)
    r = x - hi.astype(f32)
    mid = r.astype(bf16)
    lo = (r - mid.astype(f32)).astype(bf16)
    return hi, mid, lo


def _sigmoid(x):
    return 1.0 / (1.0 + jnp.exp(-x))


def _rope(x, cos, sin_a, sin_b):
    w = x.shape[-1]
    xl = pltpu.roll(x, w - 16, 1)
    xr = pltpu.roll(x, 16, 1)
    return x * cos + xl * sin_a + xr * sin_b


def _pad_heads(x, n_heads, fill):
    pieces = []
    for hh in range(n_heads):
        pieces += [x[:, hh * ATT_HEAD_DIM:(hh + 1) * ATT_HEAD_DIM], fill]
    return jnp.concatenate(pieces, axis=1)


def _in_proj_kernel(h_ref, g_ref, w_ref, cos_ref, sa_ref, sb_ref, qg_ref, kg_ref, seg_ref,
                    u_ref, mqk_ref, mv_ref, mo_ref, gt_ref, aq_ref, ak_ref, av_ref):
    h = h_ref[...]
    xn = h * lax.rsqrt(jnp.mean(h * h, axis=-1, keepdims=True) + EPS) * g_ref[...]
    p = _dot(xn.astype(bf16), w_ref[...])
    u_ref[...] = p[:, C_U:C_MQK]
    mqk_ref[...] = p[:, C_MQK:C_MV]
    mv_ref[...] = p[:, C_MV:C_MO].astype(bf16)
    mo_ref[...] = p[:, C_MO:C_G].astype(bf16)
    gt_ref[...] = p[:, C_G:C_AQ]
    cos = cos_ref[...]
    sa = sa_ref[...]
    sb = sb_ref[...]
    seg = seg_ref[...]
    q = p[:, C_AQ:C_AK]
    qn = q * lax.rsqrt(_split2_dot(q * q, seg) + EPS) * qg_ref[...]
    rep = lambda t: jnp.concatenate([t] * (D_ATTN // LANE), axis=1)
    qr = _rope(qn, rep(cos), rep(sa), rep(sb)) * (ATT_HEAD_DIM ** -0.5 * LOG2E)
    k = p[:, C_AK:C_AV]
    kn = k * lax.rsqrt(_split2_dot(k * k, seg[:ATT_KV_W, :ATT_KV_W]) + EPS) * kg_ref[...]
    zeros = jnp.zeros((h.shape[0], ATT_HEAD_DIM), f32)
    one0 = (lax.broadcasted_iota(jnp.int32, zeros.shape, 1) == 0).astype(f32)
    aq_ref[...] = _pad_heads(qr, ATT_HEADS, zeros).astype(bf16)
    ak_ref[...] = _pad_heads(_rope(kn, cos, sa, sb), ATT_KV_HEADS, zeros).astype(bf16)
    av_ref[...] = _pad_heads(p[:, C_AV:D_INP], ATT_KV_HEADS, one0).astype(bf16)


def _in_proj(h, g, w, cos, sa, sb, qg, kg, seg, tm):
    B, Lp, D = h.shape
    nt = Lp // tm
    row = lambda w_: pl.BlockSpec((None, tm, w_), lambda b, i: (b, i, 0))
    full = lambda a: pl.BlockSpec(a.shape, lambda b, i: (0,) * a.ndim)
    tab = pl.BlockSpec((tm, LANE), lambda b, i: (i, 0))
    return pl.pallas_call(
        _in_proj_kernel,
        grid=(B, nt),
        in_specs=[row(D), full(g), full(w), tab, tab, tab, full(qg), full(kg), full(seg)],
        out_specs=[
            row(D_SSM), row(2 * D_MLSTM), row(D_MLSTM), row(D_MLSTM), row(LANE),
            row(ATT_HEADS * LANE), row(ATT_KV_HEADS * LANE), row(ATT_KV_HEADS * LANE),
        ],
        out_shape=[
            jax.ShapeDtypeStruct((B, Lp, D_SSM), f32),
            jax.ShapeDtypeStruct((B, Lp, 2 * D_MLSTM), f32),
            jax.ShapeDtypeStruct((B, Lp, D_MLSTM), bf16),
            jax.ShapeDtypeStruct((B, Lp, D_MLSTM), bf16),
            jax.ShapeDtypeStruct((B, Lp, LANE), f32),
            jax.ShapeDtypeStruct((B, Lp, ATT_HEADS * LANE), bf16),
            jax.ShapeDtypeStruct((B, Lp, ATT_KV_HEADS * LANE), bf16),
            jax.ShapeDtypeStruct((B, Lp, ATT_KV_HEADS * LANE), bf16),
        ],
        compiler_params=pltpu.CompilerParams(
            dimension_semantics=("parallel", "arbitrary"), vmem_limit_bytes=VMEM_LIMIT),
        name="in_proj",
    )(h, g, w, cos, sa, sb, qg, kg, seg)


def _s5_kernel(u_ref, bm_ref, cm_ref, are_ref, aim_ref, y_ref, x_ref, st_ref, io_ref, *, tc, nb):
    d = pl.program_id(0)
    c = pl.program_id(1)

    @pl.when(c == 0)
    def _():
        st_ref[...] = jnp.zeros_like(st_ref)

    halves = D_SSM // LANE
    for b in range(nb):
        for hh in range(halves):
            io_ref[hh, pl.ds(b, tc, stride=nb), :] = u_ref[b, :, hh * LANE:(hh + 1) * LANE]
    u = jnp.concatenate([io_ref[hh] for hh in range(halves)], axis=1)
    x_ref[...] = _dot(u.astype(bf16), bm_ref[...])
    lw = 512
    for lb in range(SSM_LANES // lw):
        re_sl = pl.ds(lb * lw, lw)
        im_sl = pl.ds(SSM_LANES + lb * lw, lw)
        a_re = jnp.broadcast_to(are_ref[:, lb * lw:(lb + 1) * lw], (nb, lw))
        a_im = jnp.broadcast_to(aim_ref[:, lb * lw:(lb + 1) * lw], (nb, lw))

        def step(t, carry):
            xr, xi = carry
            tt = d * (tc - 1 - t) + (1 - d) * t
            r = pl.multiple_of(tt * nb, 8)
            nr = a_re * xr - a_im * xi + x_ref[pl.ds(r, nb), re_sl]
            ni = a_re * xi + a_im * xr + x_ref[pl.ds(r, nb), im_sl]
            x_ref[pl.ds(r, nb), re_sl] = nr
            x_ref[pl.ds(r, nb), im_sl] = ni
            return nr, ni

        xr, xi = lax.fori_loop(0, tc, step, (st_ref[:, re_sl], st_ref[:, im_sl]), unroll=4)
        st_ref[:, re_sl] = xr
        st_ref[:, im_sl] = xi
    y = _dot(x_ref[...].astype(bf16), cm_ref[...])
    for hh in range(halves):
        io_ref[hh] = y[:, hh * LANE:(hh + 1) * LANE]
    for b in range(nb):
        y_ref[b] = jnp.concatenate([io_ref[hh, pl.ds(b, tc, stride=nb), :] for hh in range(halves)], axis=1)


def _s5_scan(u, bmat, cmat, a_re, a_im):
    nb, Lp, _ = u.shape
    tc = S5_CHUNK
    nc = Lp // tc
    cidx = lambda d, c: d * (nc - 1 - c) + (1 - d) * c
    return pl.pallas_call(
        functools.partial(_s5_kernel, tc=tc, nb=nb),
        grid=(2, nc),
        in_specs=[
            pl.BlockSpec((nb, tc, D_SSM), lambda d, c: (0, cidx(d, c), 0)),
            pl.BlockSpec((None, D_SSM, 2 * SSM_LANES), lambda d, c: (d, 0, 0)),
            pl.BlockSpec((None, 2 * SSM_LANES, D_SSM), lambda d, c: (d, 0, 0)),
            pl.BlockSpec((None, 1, SSM_LANES), lambda d, c: (d, 0, 0)),
            pl.BlockSpec((None, 1, SSM_LANES), lambda d, c: (d, 0, 0)),
        ],
        out_specs=pl.BlockSpec((None, nb, tc, D_SSM), lambda d, c: (d, 0, cidx(d, c), 0)),
        out_shape=jax.ShapeDtypeStruct((2, nb, Lp, D_SSM), f32),
        scratch_shapes=[pltpu.VMEM((tc * nb, 2 * SSM_LANES), f32),
                        pltpu.VMEM((nb, 2 * SSM_LANES), f32),
                        pltpu.VMEM((D_SSM // LANE, tc * nb, LANE), f32)],
        compiler_params=pltpu.CompilerParams(
            dimension_semantics=("arbitrary", "arbitrary"), vmem_limit_bytes=VMEM_LIMIT),
        name="s5_scan",
    )(u, bmat, cmat, a_re, a_im)


def _conv3(x, prev8, next8, w_ref, b_ref, first, last):
    n = x.shape[0]
    rows = lax.broadcasted_iota(jnp.int32, x.shape, 0)
    pv = jnp.where(first, 0.0, prev8[7:8, :])
    nx = jnp.where(last, 0.0, next8[0:1, :])
    xm = jnp.where(rows == 0, pv, pltpu.roll(x, 1, 0))
    xp = jnp.where(rows == n - 1, nx, pltpu.roll(x, n - 1, 0))
    return xm * w_ref[0:1, :] + x * w_ref[1:2, :] + xp * w_ref[2:3, :] + b_ref[...]


def _mlstm_gates(g, chunk, bias, *, reverse):
    lc = ML_CHUNK
    r_i = lax.broadcasted_iota(jnp.int32, (lc, lc), 0)
    c_i = lax.broadcasted_iota(jnp.int32, (lc, lc), 1)
    tri = ((c_i >= r_i) if reverse else (c_i <= r_i)).astype(bf16)
    gb = g + bias
    pos = chunk * lc + lax.broadcasted_iota(jnp.int32, (lc, LANE), 0)
    valid = pos >= PAD_FRONT
    li = jnp.where(valid, gb, NEG)
    lf = jnp.where(valid, jnp.minimum(gb, 0.0) - jnp.log(1.0 + jnp.exp(-jnp.abs(gb))), 0.0)
    hi, mid, lo = _split3(lf)
    bcol = _dot(tri, hi) + _dot(tri, mid) + _dot(tri, lo)
    return li, li.T, bcol, bcol.T


def _bmm(a, b):
    return lax.dot_general(a, b, (((2,), (1,)), ((0,), (0,))), preferred_element_type=f32)


def _bmm_nt(a, b):
    return lax.dot_general(a, b, (((2,), (2,)), ((0,), (0,))), preferred_element_type=f32)


def _bmm_tn(a, b):
    return lax.dot_general(a, b, (((1,), (1,)), ((0,), (0,))), preferred_element_type=f32)


def _mlstm_kernel(qkf_ref, qkfp_ref, qkfn_ref, vf_ref, gf_ref,
                  qkb_ref, qkbp_ref, qkbn_ref, vb_ref, gb_ref,
                  cw_ref, cb_ref, bias_ref, hf_ref, hb_ref, ct_ref, n_ref, m_ref, *, nc):
    c = pl.program_id(1)
    lc = ML_CHUNK
    H = ML_HEADS
    nb = 2 * H

    @pl.when(c == 0)
    def _():
        ct_ref[...] = jnp.zeros_like(ct_ref)
        n_ref[...] = jnp.zeros_like(n_ref)
        m_ref[...] = jnp.zeros_like(m_ref)

    cb = nc - 1 - c
    bias = bias_ref[...]
    qk_f = _conv3(qkf_ref[...], qkfp_ref[...], qkfn_ref[...], cw_ref, cb_ref, c == 0, c == nc - 1)
    qk_b = _conv3(qkb_ref[...], qkbp_ref[...], qkbn_ref[...], cw_ref, cb_ref, cb == 0, cb == nc - 1)
    qs, ks, vs, bcs, brs, lirs, lics, bends = [], [], [], [], [], [], [], []
    for d, (qk, v, g, chunk) in enumerate(((qk_f, vf_ref[...], gf_ref[...], c),
                                           (qk_b, vb_ref[...], gb_ref[...], cb))):
        li, li_t, bcol, b_t = _mlstm_gates(g, chunk, bias, reverse=bool(d))
        q_all = qk[:, :D_MLSTM].astype(bf16)
        k_all = (qk[:, D_MLSTM:] * (ML_HEAD_DIM ** -0.5)).astype(bf16)
        e_row = 0 if d else lc - 1
        for hh in range(H):
            ci = d * H + hh
            cf = nb + ci
            sl = slice(hh * ML_HEAD_DIM, (hh + 1) * ML_HEAD_DIM)
            qs.append(q_all[:, sl])
            ks.append(k_all[:, sl])
            vs.append(v[:, sl])
            bcs.append(bcol[:, cf:cf + 1])
            brs.append(b_t[cf:cf + 1, :])
            lirs.append(li_t[ci:ci + 1, :])
            lics.append(li[:, ci:ci + 1])
            bends.append(b_t[cf:cf + 1, e_row:e_row + 1])
    q, k, v = jnp.stack(qs), jnp.stack(ks), jnp.stack(vs)
    bc, br = jnp.stack(bcs), jnp.stack(brs)
    lir, lic, b_end = jnp.stack(lirs), jnp.stack(lics), jnp.stack(bends)
    ct, n, m_prev = ct_ref[...], n_ref[...], m_ref[...][:, :, 0:1]
    bi = lax.broadcasted_iota(jnp.int32, (nb, lc, lc), 0)
    r_i = lax.broadcasted_iota(jnp.int32, (nb, lc, lc), 1)
    c_i = lax.broadcasted_iota(jnp.int32, (nb, lc, lc), 2)
    keep = jnp.where(bi < H, c_i - r_i, r_i - c_i) <= 0
    dm = jnp.where(keep, bc - br + lir, NEG)
    a = bc + m_prev
    m_t = jnp.maximum(a, jnp.max(dm, axis=2, keepdims=True))
    s = _bmm_nt(q, k) * jnp.exp(dm - m_t)
    e = jnp.exp(a - m_t)
    num = _bmm(s.astype(bf16), v) + e * _bmm(q, ct.astype(bf16))
    den = jnp.sum(s, axis=2, keepdims=True) + e * jnp.sum(q.astype(f32) * n, axis=2, keepdims=True)
    h = num / jnp.maximum(jnp.abs(den), jnp.exp(-m_t))
    hf_ref[...] = jnp.concatenate([h[i] for i in range(H)], axis=1)
    hb_ref[...] = jnp.concatenate([h[H + i] for i in range(H)], axis=1)
    m_loc = jnp.max(b_end - br + lir, axis=2, keepdims=True)
    kw = k.astype(f32) * jnp.exp(b_end - bc + lic - m_loc)
    ct_loc = _bmm_tn(kw.astype(bf16), v)
    n_loc = jnp.sum(kw, axis=1, keepdims=True)
    m_new = jnp.maximum(b_end + m_prev, m_loc)
    s_old = jnp.exp(b_end + m_prev - m_new)
    s_loc = jnp.exp(m_loc - m_new)
    ct_ref[...] = s_old * ct + s_loc * ct_loc
    n_ref[...] = s_old * n + s_loc * n_loc
    m_ref[...] = jnp.broadcast_to(m_new, (nb, 1, LANE))


def _mlstm(mqk, mv, gates, conv_w, conv_b, bias):
    B, Lp, _ = mqk.shape
    lc = ML_CHUNK
    nc = Lp // lc
    n8 = Lp // 8
    per = lc // 8
    fwd = lambda b, c: (b, c, 0)
    bwd = lambda b, c: (b, nc - 1 - c, 0)
    prev = lambda f: (lambda b, c: (b, jnp.maximum(f(b, c)[1] * per - 1, 0), 0))
    nxt = lambda f: (lambda b, c: (b, jnp.minimum((f(b, c)[1] + 1) * per, n8 - 1), 0))
    full = lambda a: pl.BlockSpec(a.shape, lambda b, c: (0,) * a.ndim)
    w2 = 2 * D_MLSTM

    def specs(f):
        return [pl.BlockSpec((None, lc, w2), f), pl.BlockSpec((None, 8, w2), prev(f)),
                pl.BlockSpec((None, 8, w2), nxt(f)), pl.BlockSpec((None, lc, D_MLSTM), f),
                pl.BlockSpec((None, lc, LANE), f)]

    return pl.pallas_call(
        functools.partial(_mlstm_kernel, nc=nc),
        grid=(B, nc),
        in_specs=specs(fwd) + specs(bwd) + [full(conv_w), full(conv_b), full(bias)],
        out_specs=[pl.BlockSpec((None, lc, D_MLSTM), fwd), pl.BlockSpec((None, lc, D_MLSTM), bwd)],
        out_shape=[jax.ShapeDtypeStruct((B, Lp, D_MLSTM), f32)] * 2,
        scratch_shapes=[pltpu.VMEM((2 * ML_HEADS, ML_HEAD_DIM, ML_HEAD_DIM), f32),
                        pltpu.VMEM((2 * ML_HEADS, 1, ML_HEAD_DIM), f32),
                        pltpu.VMEM((2 * ML_HEADS, 1, LANE), f32)],
        compiler_params=pltpu.CompilerParams(
            dimension_semantics=("parallel", "arbitrary"), vmem_limit_bytes=VMEM_LIMIT),
        name="mlstm",
    )(mqk, mqk, mqk, mv, gates, mqk, mqk, mqk, mv, gates, conv_w, conv_b, bias)


def _attn_kernel(q_ref, k_ref, v_ref, kb_ref, o_ref, s_ref, *, tk):
    tq = q_ref.shape[0]
    lp = k_ref.shape[0]
    grp = ATT_HEADS // ATT_KV_HEADS
    tiles = [(0, HEAD_ROWS)] + [(c, tk) for c in range(HEAD_ROWS, lp, tk)]
    outs = []
    for g in range(ATT_KV_HEADS):
        gl = slice(g * LANE, (g + 1) * LANE)
        q4 = jnp.concatenate(
            [q_ref[:, (g * grp + j) * LANE:(g * grp + j + 1) * LANE] for j in range(grp)], axis=0)
        m = None
        for c0, n in tiles:
            s = _dot_t(q4, k_ref[c0:c0 + n, gl])
            if c0 == 0:
                s = s + kb_ref[...]
            s_ref[:, c0:c0 + n] = s
            tm_ = jnp.max(s, axis=1, keepdims=True)
            m = tm_ if m is None else jnp.maximum(m, tm_)
        acc = None
        for c0, n in tiles:
            p = jnp.exp2(s_ref[:, c0:c0 + n] - m).astype(bf16)
            part = _dot(p, v_ref[c0:c0 + n, gl])
            acc = part if acc is None else acc + part
        o = acc[:, :ATT_HEAD_DIM] / acc[:, ATT_HEAD_DIM:ATT_HEAD_DIM + 1]
        outs.extend(o[j * tq:(j + 1) * tq] for j in range(grp))
    o_ref[...] = jnp.concatenate(outs, axis=1).astype(bf16)


def _attention(aq, ak, av, kbias):
    B, Lp, _ = aq.shape
    tq = ATT_TQ
    n_real = Lp - HEAD_ROWS
    tk = n_real // 2 if n_real % (2 * LANE) == 0 else n_real
    grp = ATT_HEADS // ATT_KV_HEADS
    return pl.pallas_call(
        functools.partial(_attn_kernel, tk=tk),
        grid=(B, Lp // tq),
        in_specs=[pl.BlockSpec((None, tq, ATT_HEADS * LANE), lambda b, i: (b, i, 0)),
                  pl.BlockSpec((None, Lp, ATT_KV_HEADS * LANE), lambda b, i: (b, 0, 0)),
                  pl.BlockSpec((None, Lp, ATT_KV_HEADS * LANE), lambda b, i: (b, 0, 0)),
                  pl.BlockSpec((1, HEAD_ROWS), lambda b, i: (0, 0))],
        out_specs=pl.BlockSpec((None, tq, D_ATTN), lambda b, i: (b, i, 0)),
        out_shape=jax.ShapeDtypeStruct((B, Lp, D_ATTN), bf16),
        scratch_shapes=[pltpu.VMEM((grp * tq, Lp), f32)],
        compiler_params=pltpu.CompilerParams(
            dimension_semantics=("parallel", "arbitrary"), vmem_limit_bytes=VMEM_LIMIT),
        name="attention",
    )(aq, ak, av, kbias)


def _out_proj_kernel(h_ref, ys_ref, u_ref, hf_ref, hb_ref, mo_ref, att_ref,
                     dsk_ref, wglu_ref, mlg_ref, seg_ref, wout_ref, g2_ref, wr_ref,
                     hn_ref, xn_ref, gate_ref, *, tm, with_router):
    i = pl.program_id(1)
    u = u_ref[...]
    y = ys_ref[0] + ys_ref[1] + dsk_ref[...] * u
    y = 0.5 * y * (1.0 + jnp.tanh(math.sqrt(2.0 / math.pi) * (y + 0.044715 * (y * y * y))))
    ag = _dot(y.astype(bf16), wglu_ref[...])
    y_ssm = ag[:, :D_SSM] * _sigmoid(ag[:, D_SSM:])
    hm = hf_ref[...] + hb_ref[...]
    hn = hm * lax.rsqrt(_split2_dot(hm * hm, seg_ref[...]) + EPS) * mlg_ref[...]
    y_ml = _sigmoid(mo_ref[...].astype(f32)) * hn
    acc = (_dot(y_ssm.astype(bf16), wout_ref[0:D_SSM, :])
           + _dot(y_ml.astype(bf16), wout_ref[D_SSM:D_SSM + D_MLSTM, :])
           + _dot(att_ref[...], wout_ref[D_SSM + D_MLSTM:, :]))
    pos = i * tm + lax.broadcasted_iota(jnp.int32, (tm, 1), 0)
    h_new = jnp.where(pos >= PAD_FRONT, h_ref[...] + acc, 0.0)
    hn_ref[...] = h_new
    xn = h_new * lax.rsqrt(jnp.mean(h_new * h_new, axis=-1, keepdims=True) + EPS) * g2_ref[...]
    x_hi = xn.astype(bf16)
    xn_ref[...] = x_hi
    if not with_router:
        gate_ref[...] = jnp.zeros_like(gate_ref)
        return
    x_lo = (xn - x_hi.astype(f32)).astype(bf16)
    logits = _dot(x_hi, wr_ref[0]) + _dot(x_lo, wr_ref[0]) + _dot(x_hi, wr_ref[1])
    lane = lax.broadcasted_iota(jnp.int32, logits.shape, 1).astype(f32)
    lg = jnp.where(lane < N_EXPERTS, logits, NEG)
    v1 = jnp.max(lg, axis=1, keepdims=True)
    i1 = jnp.min(jnp.where(lg == v1, lane, float(LANE)), axis=1, keepdims=True)
    lg2 = jnp.where(lane == i1, NEG, lg)
    v2 = jnp.max(lg2, axis=1, keepdims=True)
    i2 = jnp.min(jnp.where(lg2 == v2, lane, float(LANE)), axis=1, keepdims=True)
    g1 = 1.0 / (1.0 + jnp.exp(v2 - v1))
    gate_ref[...] = jnp.where(lane == 0.0, i1, jnp.where(lane == 1.0, i2, jnp.where(
        lane == 2.0, g1, jnp.where(lane == 3.0, 1.0 - g1, 0.0))))


def _out_proj(h, ys, u, hf, hb, mo, att, dsk, wglu, mlg, seg, wout, g2, wr, tm, with_router):
    B, Lp, D = h.shape
    row = lambda w_: pl.BlockSpec((None, tm, w_), lambda b, i: (b, i, 0))
    full = lambda a: pl.BlockSpec(a.shape, lambda b, i: (0,) * a.ndim)
    return pl.pallas_call(
        functools.partial(_out_proj_kernel, tm=tm, with_router=with_router),
        grid=(B, Lp // tm),
        in_specs=[row(D),
                  pl.BlockSpec((2, None, tm, D_SSM), lambda b, i: (0, b, i, 0)),
                  row(D_SSM), row(D_MLSTM), row(D_MLSTM), row(D_MLSTM), row(D_ATTN),
                  full(dsk), full(wglu), full(mlg), full(seg), full(wout), full(g2), full(wr)],
        out_specs=[row(D), row(D), row(LANE)],
        out_shape=[jax.ShapeDtypeStruct((B, Lp, D), f32),
                   jax.ShapeDtypeStruct((B, Lp, D), bf16),
                   jax.ShapeDtypeStruct((B, Lp, LANE), f32)],
        compiler_params=pltpu.CompilerParams(
            dimension_semantics=("parallel", "arbitrary"), vmem_limit_bytes=VMEM_LIMIT),
        name="out_proj",
    )(h, ys, u, hf, hb, mo, att, dsk, wglu, mlg, seg, wout, g2, wr)


def _swiglu_part(x, w1, w3, w2):
    a = _dot(x, w1)
    b = _dot(x, w3)
    return _dot((a * _sigmoid(a) * b).astype(bf16), w2)


def _final_norm(h, gf):
    return h * lax.rsqrt(jnp.mean(h * h, axis=-1, keepdims=True) + EPS) * gf


def _ffn_kernel(x_ref, h_ref, w1_ref, w3_ref, w2_ref, gf_ref, o_ref, *, final):
    j = pl.program_id(1)
    part = _swiglu_part(x_ref[...], w1_ref[...], w3_ref[...], w2_ref[...])

    @pl.when(j == 0)
    def _():
        o_ref[...] = h_ref[...] + part

    @pl.when(j > 0)
    def _():
        o_ref[...] += part

    if final:
        @pl.when(j == pl.num_programs(1) - 1)
        def _():
            o_ref[...] = _final_norm(o_ref[...], gf_ref[...])


def _ffn(x, h, w1, w3, w2, gf, tm, final):
    T, D = h.shape
    nj = D_FF // FF_HALF
    return pl.pallas_call(
        functools.partial(_ffn_kernel, final=final),
        grid=(T // tm, nj),
        in_specs=[pl.BlockSpec((tm, D), lambda i, j: (i, 0)),
                  pl.BlockSpec((tm, D), lambda i, j: (i, 0)),
                  pl.BlockSpec((D, FF_HALF), lambda i, j: (0, j)),
                  pl.BlockSpec((D, FF_HALF), lambda i, j: (0, j)),
                  pl.BlockSpec((FF_HALF, D), lambda i, j: (j, 0)),
                  pl.BlockSpec((1, D), lambda i, j: (0, 0))],
        out_specs=pl.BlockSpec((tm, D), lambda i, j: (i, 0)),
        out_shape=jax.ShapeDtypeStruct((T, D), f32),
        compiler_params=pltpu.CompilerParams(
            dimension_semantics=("parallel", "arbitrary"), vmem_limit_bytes=VMEM_LIMIT),
        name="ffn",
    )(x, h, w1, w3, w2, gf)


def _route_kernel(r_ref, pos_ref, cnt_ref, run_ref, *, tg):
    p = pl.program_id(0)
    i = pl.program_id(1)
    tr = r_ref.shape[0]
    r = r_ref[...]
    lane = lax.broadcasted_iota(jnp.int32, (tr, LANE), 1).astype(f32)
    oh1 = (lane == r[:, 0:1]).astype(f32)
    oh2 = (lane == r[:, 1:2]).astype(f32)
    both = oh1 + oh2
    tile_cnt = jnp.sum(both, axis=0, keepdims=True)

    @pl.when(jnp.logical_and(p == 0, i == 0))
    def _():
        run_ref[...] = jnp.zeros_like(run_ref)

    @pl.when(p == 0)
    def _():
        run_ref[...] += tile_cnt

    @pl.when(jnp.logical_and(p == 1, i == 0))
    def _():
        cnt = run_ref[...]
        cnt_ref[...] = cnt.astype(jnp.int32)
        padded = jnp.broadcast_to(jnp.ceil(cnt * (1.0 / tg)) * tg, (8, LANE))
        e_r = lax.broadcasted_iota(jnp.int32, (LANE, LANE), 0)
        e_c = lax.broadcasted_iota(jnp.int32, (LANE, LANE), 1)
        before = (e_r < e_c).astype(bf16)
        hi, mid, lo = _split3(padded)
        run_ref[...] = (_dot(hi, before) + _dot(mid, before) + _dot(lo, before))[0:1, :]

    @pl.when(p == 1)
    def _():
        t_r = lax.broadcasted_iota(jnp.int32, (tr, tr), 0)
        t_c = lax.broadcasted_iota(jnp.int32, (tr, tr), 1)
        earlier = (t_c < t_r).astype(bf16)
        base = run_ref[...] + _dot(earlier, both.astype(bf16))
        p1 = jnp.sum(oh1 * base, axis=1, keepdims=True)
        p2 = jnp.sum(oh2 * base, axis=1, keepdims=True)
        pos_ref[...] = jnp.where(lane == 0.0, p1, jnp.where(lane == 1.0, p2, 0.0)).astype(jnp.int32)
        run_ref[...] += tile_cnt


def _route(route, tg):
    T = route.shape[0]
    tr = _row_tile(T)
    return pl.pallas_call(
        functools.partial(_route_kernel, tg=tg),
        grid=(2, T // tr),
        in_specs=[pl.BlockSpec((tr, LANE), lambda p, i: (i, 0))],
        out_specs=[pl.BlockSpec((tr, LANE), lambda p, i: (i * p, 0)),
                   pl.BlockSpec((1, LANE), lambda p, i: (0, 0))],
        out_shape=[jax.ShapeDtypeStruct((T, LANE), jnp.int32),
                   jax.ShapeDtypeStruct((1, LANE), jnp.int32)],
        scratch_shapes=[pltpu.VMEM((1, LANE), f32)],
        compiler_params=pltpu.CompilerParams(
            dimension_semantics=("arbitrary", "arbitrary"), vmem_limit_bytes=VMEM_LIMIT),
        name="moe_route",
    )(route)


def _slab_rows(ref, row, n=1):
    return ref.at[pl.ds(pl.multiple_of(row * SLAB, SLAB), n * SLAB)]


def _dispatch_kernel(pos_ref, x_ref, xs_in_ref, xs_ref, buf_ref, sem, *, td, T):
    del xs_in_ref
    base = pl.program_id(0) * td
    xf = x_ref[...].astype(f32)
    for s in range(SLAB):
        buf_ref[pl.ds(s, td, stride=SLAB), :] = xf[:, s * LANE:(s + 1) * LANE]

    def copies(r):
        src = _slab_rows(buf_ref, r)
        return (pltpu.make_async_copy(src, _slab_rows(xs_ref, pos_ref[base + r]), sem.at[0]),
                pltpu.make_async_copy(src, _slab_rows(xs_ref, pos_ref[T + base + r]), sem.at[1]))

    def issue(r, carry):
        for cp in copies(r):
            cp.start()
        return carry

    def drain(r, carry):
        for cp in copies(r):
            cp.wait()
        return carry

    lax.fori_loop(0, td, issue, 0, unroll=8)
    lax.fori_loop(0, td, drain, 0, unroll=8)


def _dispatch(pos_flat, x, n_rows, td):
    T, D = x.shape
    xs0 = jnp.zeros((n_rows * SLAB, LANE), f32)
    return pl.pallas_call(
        functools.partial(_dispatch_kernel, td=td, T=T),
        grid_spec=pltpu.PrefetchScalarGridSpec(
            num_scalar_prefetch=1,
            grid=(T // td,),
            in_specs=[pl.BlockSpec((td, D), lambda i, pos: (i, 0)),
                      pl.BlockSpec(memory_space=pl.ANY)],
            out_specs=pl.BlockSpec(memory_space=pl.ANY),
            scratch_shapes=[pltpu.VMEM((td * SLAB, LANE), f32), pltpu.SemaphoreType.DMA((2,))]),
        out_shape=jax.ShapeDtypeStruct((n_rows * SLAB, LANE), f32),
        input_output_aliases={2: 0},
        compiler_params=pltpu.CompilerParams(
            dimension_semantics=("arbitrary",), vmem_limit_bytes=VMEM_LIMIT),
        name="moe_dispatch",
    )(pos_flat, x, xs0)


def _gmm_kernel(te_ref, nu_ref, xs_ref, w1_ref, w3_ref, w2_ref, ys_ref, acc_ref, *, tg):
    j = pl.program_id(0)
    f = pl.program_id(1)
    nf = pl.num_programs(1)

    @pl.when(j < nu_ref[0])
    def _():
        x = jnp.concatenate([xs_ref[pl.ds(s, tg, stride=SLAB), :] for s in range(SLAB)], axis=1)
        part = _swiglu_part(x.astype(bf16), w1_ref[...], w3_ref[...], w2_ref[...])

        @pl.when(f == 0)
        def _():
            acc_ref[...] = part

        @pl.when(jnp.logical_and(f > 0, f < nf - 1))
        def _():
            acc_ref[...] += part

        @pl.when(f == nf - 1)
        def _():
            tot = acc_ref[...] + part
            for s in range(SLAB):
                ys_ref[pl.ds(s, tg, stride=SLAB), :] = tot[:, s * LANE:(s + 1) * LANE]

    @pl.when(jnp.logical_and(j >= nu_ref[0], f == nf - 1))
    def _():
        ys_ref[...] = jnp.zeros_like(ys_ref)


def _gmm(tile_expert, n_used, xs, w1, w3, w2, tg):
    n_tiles = xs.shape[0] // (tg * SLAB)
    nf = D_FF // FF_HALF
    tile = lambda j, f, te, nu: (jnp.minimum(j, nu[0] - 1), 0)
    return pl.pallas_call(
        functools.partial(_gmm_kernel, tg=tg),
        grid_spec=pltpu.PrefetchScalarGridSpec(
            num_scalar_prefetch=2,
            grid=(n_tiles, nf),
            in_specs=[pl.BlockSpec((tg * SLAB, LANE), tile),
                      pl.BlockSpec((None, D_MODEL, FF_HALF), lambda j, f, te, nu: (te[j], 0, f)),
                      pl.BlockSpec((None, D_MODEL, FF_HALF), lambda j, f, te, nu: (te[j], 0, f)),
                      pl.BlockSpec((None, FF_HALF, D_MODEL), lambda j, f, te, nu: (te[j], f, 0))],
            out_specs=pl.BlockSpec((tg * SLAB, LANE), lambda j, f, te, nu: (j, 0)),
            scratch_shapes=[pltpu.VMEM((tg, D_MODEL), f32)]),
        out_shape=jax.ShapeDtypeStruct(xs.shape, f32),
        compiler_params=pltpu.CompilerParams(
            dimension_semantics=("arbitrary", "arbitrary"), vmem_limit_bytes=VMEM_LIMIT),
        name="moe_gmm",
    )(tile_expert, n_used, xs, w1, w3, w2)


def _combine_kernel(pos_ref, h_ref, r_ref, ys_ref, gf_ref, o_ref, b1_ref, b2_ref, sem, *, tc, T, final):
    base = pl.program_id(0) * tc

    def copies(r):
        return (pltpu.make_async_copy(_slab_rows(ys_ref, pos_ref[base + r]), _slab_rows(b1_ref, r), sem.at[0]),
                pltpu.make_async_copy(_slab_rows(ys_ref, pos_ref[T + base + r]), _slab_rows(b2_ref, r), sem.at[1]))

    def issue(r, carry):
        for cp in copies(r):
            cp.start()
        return carry

    def drain(r, carry):
        for cp in copies(r):
            cp.wait()
        return carry

    lax.fori_loop(0, tc, issue, 0, unroll=8)
    lax.fori_loop(0, tc, drain, 0, unroll=8)
    rows = lambda b: jnp.concatenate([b[pl.ds(s, tc, stride=SLAB), :] for s in range(SLAB)], axis=1)
    r = r_ref[...]
    out = h_ref[...] + r[:, 2:3] * rows(b1_ref) + r[:, 3:4] * rows(b2_ref)
    o_ref[...] = _final_norm(out, gf_ref[...]) if final else out


def _combine(pos_flat, h, route, ys, gf, tc, final):
    T, D = h.shape
    return pl.pallas_call(
        functools.partial(_combine_kernel, tc=tc, T=T, final=final),
        grid_spec=pltpu.PrefetchScalarGridSpec(
            num_scalar_prefetch=1,
            grid=(T // tc,),
            in_specs=[pl.BlockSpec((tc, D), lambda i, pos: (i, 0)),
                      pl.BlockSpec((tc, LANE), lambda i, pos: (i, 0)),
                      pl.BlockSpec(memory_space=pl.ANY),
                      pl.BlockSpec((1, D), lambda i, pos: (0, 0))],
            out_specs=pl.BlockSpec((tc, D), lambda i, pos: (i, 0)),
            scratch_shapes=[pltpu.VMEM((tc * SLAB, LANE), f32), pltpu.VMEM((tc * SLAB, LANE), f32),
                            pltpu.SemaphoreType.DMA((2,))]),
        out_shape=jax.ShapeDtypeStruct((T, D), f32),
        compiler_params=pltpu.CompilerParams(
            dimension_semantics=("arbitrary",), vmem_limit_bytes=VMEM_LIMIT),
        name="moe_combine",
    )(pos_flat, h, route, ys, gf)


def _moe(x, h, route, w1, w3, w2, gf, final):
    T, D = h.shape
    tg = MOE_TG
    n_tiles = -(-(2 * T + N_EXPERTS * (tg - 1)) // tg)
    pos, cnt = _route(route, tg)
    ends = jnp.cumsum(-(-cnt[0, :N_EXPERTS] // tg) * tg)
    n_used = (ends[-1:] // tg).astype(jnp.int32)
    tile_expert = jnp.minimum(
        jnp.sum(ends[None, :] <= (jnp.arange(n_tiles, dtype=jnp.int32) * tg)[:, None], axis=1),
        N_EXPERTS - 1).astype(jnp.int32)
    pos_flat = jnp.concatenate([pos[:, 0], pos[:, 1]])
    xs = _dispatch(pos_flat, x, n_tiles * tg, MOE_TROW)
    ys = _gmm(tile_expert, n_used, xs, w1, w3, w2, tg)
    return _combine(pos_flat, h, route, ys, gf, MOE_TROW, final)


def _s5_discretise(lam_re, lam_im, log_step, b_re, b_im, c_re, c_im):
    G, P, C = SSM_GROUPS, SSM_STATE, SSM_GROUP
    dt = jnp.exp(log_step)[..., None]
    mag = jnp.exp(lam_re * dt)
    ab_re = mag * jnp.cos(lam_im * dt)
    ab_im = mag * jnp.sin(lam_im * dt)
    den = lam_re * lam_re + lam_im * lam_im
    nr = ab_re - 1.0
    ni = ab_im
    coef_re = (nr * lam_re + ni * lam_im) / den
    coef_im = (ni * lam_re - nr * lam_im) / den
    bb_re = coef_re[..., None] * b_re - coef_im[..., None] * b_im
    bb_im = coef_re[..., None] * b_im + coef_im[..., None] * b_re
    eye = jnp.eye(G, dtype=f32)

    def bdiag_in(bb):
        return jnp.einsum('dgpc,gh->dgchp', bb, eye).reshape(2, G * C, G * P)

    def bdiag_out(cc):
        return jnp.einsum('dgcp,gh->dgphc', cc, eye).reshape(2, G * P, G * C)

    bmat = jnp.concatenate([bdiag_in(bb_re), bdiag_in(bb_im)], axis=2).astype(bf16)
    cmat = jnp.concatenate([bdiag_out(c_re), -bdiag_out(c_im)], axis=1).astype(bf16)
    return bmat, cmat, ab_re.reshape(2, 1, G * P), ab_im.reshape(2, 1, G * P)


def _rope_tables(n_tokens):
    rows = n_tokens // GRID_W
    row = jnp.concatenate([jnp.zeros((PAD_FRONT,), f32), jnp.full((N_META,), -1.0, f32),
                           jnp.repeat(jnp.arange(rows, dtype=f32), GRID_W)])
    col = jnp.concatenate([jnp.zeros((PAD_FRONT,), f32), jnp.arange(N_META, dtype=f32),
                           jnp.tile(jnp.arange(GRID_W, dtype=f32), rows)])
    inv = ROPE_THETA ** (-jnp.arange(0, ROPE_AXIS, 2, dtype=f32) / ROPE_AXIS)
    ar = row[:, None] * inv
    ac = col[:, None] * inv
    ang = jnp.concatenate([ar, ar, ac, ac], -1)
    cos, sin = jnp.cos(ang), jnp.sin(ang)
    first_half = (jnp.arange(ATT_HEAD_DIM) % ROPE_AXIS) < (ROPE_AXIS // 2)
    sa = jnp.where(first_half, -sin, 0.0)
    sb = jnp.where(first_half, 0.0, sin)
    two = lambda t: jnp.concatenate([t, t], axis=1)
    return two(cos), two(sa), two(sb)


def _seg_matrix(width, seg):
    idx = jnp.arange(width) // seg
    return (idx[:, None] == idx[None, :]).astype(f32).astype(bf16) * jnp.asarray(1.0 / seg, bf16)


def _row_tile(lp):
    best = 16
    for t in range(16, lp + 1, 16):
        if lp % t == 0 and abs(t - 512) < abs(best - 512):
            best = t
    return best


def _prep(meta_tokens, norm1_g, w_in, w_out,
          ssm_lam_re, ssm_lam_im, ssm_log_step, ssm_b_re, ssm_b_im, ssm_c_re, ssm_c_im, ssm_d, ssm_w_glu,
          ml_conv_w, ml_conv_b, ml_b_i, ml_b_f, ml_norm_g, att_q_g, att_k_g,
          norm2_g, ffn_w1, ffn_w3, ffn_w2, moe_router, moe_w1, moe_w3, moe_w2, final_g):
    depth = w_in.shape[0]
    layers = []
    for l in range(depth):
        w = w_in[l]
        wg = jnp.pad(w[:, 1280:1296], ((0, 0), (0, LANE - 16)))
        w_cat = jnp.concatenate([w[:, 0:1280], wg, w[:, 1296:]], axis=1).astype(bf16)
        bmat, cmat, a_re, a_im = _s5_discretise(ssm_lam_re[l], ssm_lam_im[l], ssm_log_step[l],
                                                ssm_b_re[l], ssm_b_im[l], ssm_c_re[l], ssm_c_im[l])
        gate_bias = jnp.pad(jnp.concatenate([ml_b_i[l].reshape(-1), ml_b_f[l].reshape(-1)]),
                            (0, LANE - 4 * ML_HEADS)).reshape(1, LANE)
        lay = dict(
            g1=norm1_g[l].reshape(1, -1), w_cat=w_cat,
            qg=jnp.tile(att_q_g[l], ATT_HEADS).reshape(1, -1),
            kg=jnp.tile(att_k_g[l], ATT_KV_HEADS).reshape(1, -1),
            bmat=bmat, cmat=cmat, a_re=a_re, a_im=a_im,
            dsk=ssm_d[l].reshape(1, -1), wglu=ssm_w_glu[l].astype(bf16),
            conv_w=ml_conv_w[l], conv_b=ml_conv_b[l].reshape(1, -1), gate_bias=gate_bias,
            mlg=ml_norm_g[l].reshape(1, -1), wout=w_out[l].astype(bf16),
            g2=norm2_g[l].reshape(1, -1),
        )
        j = l // 2
        if l % 2 == 0:
            lay.update(moe=False, wr=jnp.zeros((2, D_MODEL, LANE), bf16),
                       w1=ffn_w1[j].astype(bf16), w3=ffn_w3[j].astype(bf16), w2=ffn_w2[j].astype(bf16))
        else:
            wr = jnp.pad(moe_router[j], ((0, 0), (0, LANE - N_EXPERTS)))
            wr_hi = wr.astype(bf16)
            wr_lo = (wr - wr_hi.astype(f32)).astype(bf16)
            lay.update(moe=True, wr=jnp.stack([wr_hi, wr_lo]),
                       w1=moe_w1[j].astype(bf16), w3=moe_w3[j].astype(bf16), w2=moe_w2[j].astype(bf16))
        layers.append(lay)
    return dict(layers=layers, meta=meta_tokens, gf=final_g.reshape(1, -1),
                seg512=_seg_matrix(D_ATTN, ATT_HEAD_DIM), seg256=_seg_matrix(D_MLSTM, ML_HEAD_DIM))


def _trunk(x, P):
    B, N, D = x.shape
    Lp = N + HEAD_ROWS
    tm = _row_tile(Lp)
    h = jnp.concatenate([jnp.zeros((B, PAD_FRONT, D), x.dtype),
                         jnp.broadcast_to(P['meta'].astype(x.dtype), (B, N_META, D)), x], axis=1)
    cos, sa, sb = _rope_tables(N)
    kbias = jnp.where(jnp.arange(HEAD_ROWS) >= PAD_FRONT, 0.0, NEG).astype(f32).reshape(1, HEAD_ROWS)
    depth = len(P['layers'])
    for l, lay in enumerate(P['layers']):
        u, mqk, mv, mo, gates, aq, ak, av = _in_proj(
            h, lay['g1'], lay['w_cat'], cos, sa, sb, lay['qg'], lay['kg'], P['seg512'], tm)
        ys = _s5_scan(u, lay['bmat'], lay['cmat'], lay['a_re'], lay['a_im'])
        hf, hb = _mlstm(mqk, mv, gates, lay['conv_w'], lay['conv_b'], lay['gate_bias'])
        att = _attention(aq, ak, av, kbias)
        h, xn, gate = _out_proj(h, ys, u, hf, hb, mo, att, lay['dsk'], lay['wglu'], lay['mlg'],
                                P['seg256'], lay['wout'], lay['g2'], lay['wr'], tm, lay['moe'])
        final = l == depth - 1
        T = B * Lp
        h2, x2 = h.reshape(T, D), xn.reshape(T, D)
        tt = _row_tile(T)
        if lay['moe']:
            h2 = _moe(x2, h2, gate.reshape(T, LANE), lay['w1'], lay['w3'], lay['w2'], P['gf'], final)
        else:
            h2 = _ffn(x2, h2, lay['w1'], lay['w3'], lay['w2'], P['gf'], tt, final)
        h = h2.reshape(B, Lp, D)
    return h[:, HEAD_ROWS:]


def kernel(x_prompt, x_sample, meta_tokens, norm1_g, w_in, w_out, ssm_lam_re, ssm_lam_im, ssm_log_step, ssm_b_re, ssm_b_im, ssm_c_re, ssm_c_im, ssm_d, ssm_w_glu, ml_conv_w, ml_conv_b, ml_b_i, ml_b_f, ml_norm_g, att_q_g, att_k_g, norm2_g, ffn_w1, ffn_w3, ffn_w2, moe_router, moe_w1, moe_w3, moe_w2, final_g):
    P = _prep(meta_tokens, norm1_g, w_in, w_out,
              ssm_lam_re, ssm_lam_im, ssm_log_step, ssm_b_re, ssm_b_im, ssm_c_re, ssm_c_im, ssm_d, ssm_w_glu,
              ml_conv_w, ml_conv_b, ml_b_i, ml_b_f, ml_norm_g, att_q_g, att_k_g,
              norm2_g, ffn_w1, ffn_w3, ffn_w2, moe_router, moe_w1, moe_w3, moe_w2, final_g)
    return (_trunk(x_prompt, P), _trunk(x_sample, P))
```

```python
import functools
import math

import jax
import jax.numpy as jnp
from jax import lax
from jax.experimental import pallas as pl
from jax.experimental.pallas import tpu as pltpu

f32 = jnp.float32
bf16 = jnp.bfloat16

D_MODEL = 1024
N_META = 16
GRID_W = 64
EPS = 1e-6
D_SSM = 256
D_MLSTM = 256
D_ATTN = 512
SSM_GROUP = 16
SSM_GROUPS = 16
SSM_STATE = 64
SSM_LANES = SSM_GROUPS * SSM_STATE
ML_HEADS = 4
ML_HEAD_DIM = 64
ATT_HEADS = 8
ATT_KV_HEADS = 2
ATT_HEAD_DIM = 64
ATT_KV_W = 128
ROPE_AXIS = 32
ROPE_THETA = 10000.0
D_FF = 2816
N_EXPERTS = 8

LANE = 128
HEAD_ROWS = 128
PAD_FRONT = HEAD_ROWS - N_META
ML_CHUNK = 128
ML_SEQS = 4
S5_CHUNK = 64
ATT_TQ = 128
LOG2E = 1.4426950408889634
FF_HALF = D_FF // 2
MOE_TG = 512
MOE_TROW = 256
SLAB = 8
NEG = -1e30
VMEM_LIMIT = 48 * 1024 * 1024

C_U = 0
C_MQK = 256
C_MV = 768
C_MO = 1024
C_G = 1280
C_AQ = 1408
C_AK = 1920
C_AV = 2048
D_INP = 2176


def _dot(a, b):
    return jnp.dot(a, b, preferred_element_type=f32)


def _dot_t(a, b):
    return lax.dot_general(a, b, (((1,), (1,)), ((), ())), preferred_element_type=f32)


def _split2_dot(x, m):
    hi = x.astype(bf16)
    lo = (x - hi.astype(f32)).astype(bf16)
    return _dot(hi, m) + _dot(lo, m)


def _split3(x):
    hi = x.astype(bf16)
    r = x - hi.astype(f32)
    mid = r.astype(bf16)
    lo = (r - mid.astype(f32)).astype(bf16)
    return hi, mid, lo


def _sigmoid(x):
    return 1.0 / (1.0 + jnp.exp(-x))


def _rope(x, cos, sin_a, sin_b):
    w = x.shape[-1]
    xl = pltpu.roll(x, w - 16, 1)
    xr = pltpu.roll(x, 16, 1)
    return x * cos + xl * sin_a + xr * sin_b


def _pad_heads(x, n_heads, fill):
    pieces = []
    for hh in range(n_heads):
        pieces += [x[:, hh * ATT_HEAD_DIM:(hh + 1) * ATT_HEAD_DIM], fill]
    return jnp.concatenate(pieces, axis=1)


def _in_proj_kernel(h_ref, g_ref, w_ref, cos_ref, sa_ref, sb_ref, qg_ref, kg_ref, seg_ref,
                    u_ref, mqk_ref, mv_ref, mo_ref, gt_ref, aq_ref, ak_ref, av_ref):
    h = h_ref[...]
    xn = h * lax.rsqrt(jnp.mean(h * h, axis=-1, keepdims=True) + EPS) * g_ref[...]
    p = _dot(xn.astype(bf16), w_ref[...])
    u_ref[...] = p[:, C_U:C_MQK]
    mqk_ref[...] = p[:, C_MQK:C_MV]
    zeros = jnp.zeros((h.shape[0], ATT_HEAD_DIM), f32)
    one0 = (lax.broadcasted_iota(jnp.int32, zeros.shape, 1) == 0).astype(f32)
    mv_ref[...] = _pad_heads(p[:, C_MV:C_MO], ML_HEADS, one0).astype(bf16)
    mo_ref[...] = p[:, C_MO:C_G].astype(bf16)
    gt_ref[...] = p[:, C_G:C_AQ]
    cos = cos_ref[...]
    sa = sa_ref[...]
    sb = sb_ref[...]
    seg = seg_ref[...]
    q = p[:, C_AQ:C_AK]
    qn = q * lax.rsqrt(_split2_dot(q * q, seg) + EPS) * qg_ref[...]
    rep = lambda t: jnp.concatenate([t] * (D_ATTN // LANE), axis=1)
    qr = _rope(qn, rep(cos), rep(sa), rep(sb)) * (ATT_HEAD_DIM ** -0.5 * LOG2E)
    k = p[:, C_AK:C_AV]
    kn = k * lax.rsqrt(_split2_dot(k * k, seg[:ATT_KV_W, :ATT_KV_W]) + EPS) * kg_ref[...]
    aq_ref[...] = _pad_heads(qr, ATT_HEADS, zeros).astype(bf16)
    ak_ref[...] = _pad_heads(_rope(kn, cos, sa, sb), ATT_KV_HEADS, zeros).astype(bf16)
    av_ref[...] = _pad_heads(p[:, C_AV:D_INP], ATT_KV_HEADS, one0).astype(bf16)


def _in_proj(h, g, w, cos, sa, sb, qg, kg, seg, tm):
    B, Lp, D = h.shape
    nt = Lp // tm
    row = lambda w_: pl.BlockSpec((None, tm, w_), lambda b, i: (b, i, 0))
    full = lambda a: pl.BlockSpec(a.shape, lambda b, i: (0,) * a.ndim)
    tab = pl.BlockSpec((tm, LANE), lambda b, i: (i, 0))
    return pl.pallas_call(
        _in_proj_kernel,
        grid=(B, nt),
        in_specs=[row(D), full(g), full(w), tab, tab, tab, full(qg), full(kg), full(seg)],
        out_specs=[
            row(D_SSM), row(2 * D_MLSTM), row(ML_HEADS * LANE), row(D_MLSTM), row(LANE),
            row(ATT_HEADS * LANE), row(ATT_KV_HEADS * LANE), row(ATT_KV_HEADS * LANE),
        ],
        out_shape=[
            jax.ShapeDtypeStruct((B, Lp, D_SSM), f32),
            jax.ShapeDtypeStruct((B, Lp, 2 * D_MLSTM), f32),
            jax.ShapeDtypeStruct((B, Lp, ML_HEADS * LANE), bf16),
            jax.ShapeDtypeStruct((B, Lp, D_MLSTM), bf16),
            jax.ShapeDtypeStruct((B, Lp, LANE), f32),
            jax.ShapeDtypeStruct((B, Lp, ATT_HEADS * LANE), bf16),
            jax.ShapeDtypeStruct((B, Lp, ATT_KV_HEADS * LANE), bf16),
            jax.ShapeDtypeStruct((B, Lp, ATT_KV_HEADS * LANE), bf16),
        ],
        compiler_params=pltpu.CompilerParams(
            dimension_semantics=("parallel", "arbitrary"), vmem_limit_bytes=VMEM_LIMIT),
        name="in_proj",
    )(h, g, w, cos, sa, sb, qg, kg, seg)


def _s5_kernel(u_ref, bm_ref, cm_ref, are_ref, aim_ref, y_ref, x_ref, st_ref, io_ref, *, tc, nb):
    d = pl.program_id(0)
    c = pl.program_id(1)

    @pl.when(c == 0)
    def _():
        st_ref[...] = jnp.zeros_like(st_ref)

    halves = D_SSM // LANE
    for b in range(nb):
        for hh in range(halves):
            io_ref[hh, pl.ds(b, tc, stride=nb), :] = u_ref[b, :, hh * LANE:(hh + 1) * LANE]
    u = jnp.concatenate([io_ref[hh] for hh in range(halves)], axis=1)
    x_ref[...] = _dot(u.astype(bf16), bm_ref[...])
    lw = 512
    for lb in range(SSM_LANES // lw):
        re_sl = pl.ds(lb * lw, lw)
        im_sl = pl.ds(SSM_LANES + lb * lw, lw)
        a_re = jnp.broadcast_to(are_ref[:, lb * lw:(lb + 1) * lw], (nb, lw))
        a_im = jnp.broadcast_to(aim_ref[:, lb * lw:(lb + 1) * lw], (nb, lw))

        def step(t, carry):
            xr, xi = carry
            tt = d * (tc - 1 - t) + (1 - d) * t
            r = pl.multiple_of(tt * nb, 8)
            nr = a_re * xr - a_im * xi + x_ref[pl.ds(r, nb), re_sl]
            ni = a_re * xi + a_im * xr + x_ref[pl.ds(r, nb), im_sl]
            x_ref[pl.ds(r, nb), re_sl] = nr
            x_ref[pl.ds(r, nb), im_sl] = ni
            return nr, ni

        xr, xi = lax.fori_loop(0, tc, step, (st_ref[:, re_sl], st_ref[:, im_sl]), unroll=4)
        st_ref[:, re_sl] = xr
        st_ref[:, im_sl] = xi
    y = _dot(x_ref[...].astype(bf16), cm_ref[...])
    for hh in range(halves):
        io_ref[hh] = y[:, hh * LANE:(hh + 1) * LANE]
    for b in range(nb):
        y_ref[b] = jnp.concatenate([io_ref[hh, pl.ds(b, tc, stride=nb), :] for hh in range(halves)], axis=1)


def _s5_scan(u, bmat, cmat, a_re, a_im):
    nb, Lp, _ = u.shape
    tc = S5_CHUNK
    nc = Lp // tc
    cidx = lambda d, c: d * (nc - 1 - c) + (1 - d) * c
    return pl.pallas_call(
        functools.partial(_s5_kernel, tc=tc, nb=nb),
        grid=(2, nc),
        in_specs=[
            pl.BlockSpec((nb, tc, D_SSM), lambda d, c: (0, cidx(d, c), 0)),
            pl.BlockSpec((None, D_SSM, 2 * SSM_LANES), lambda d, c: (d, 0, 0)),
            pl.BlockSpec((None, 2 * SSM_LANES, D_SSM), lambda d, c: (d, 0, 0)),
            pl.BlockSpec((None, 1, SSM_LANES), lambda d, c: (d, 0, 0)),
            pl.BlockSpec((None, 1, SSM_LANES), lambda d, c: (d, 0, 0)),
        ],
        out_specs=pl.BlockSpec((None, nb, tc, D_SSM), lambda d, c: (d, 0, cidx(d, c), 0)),
        out_shape=jax.ShapeDtypeStruct((2, nb, Lp, D_SSM), f32),
        scratch_shapes=[pltpu.VMEM((tc * nb, 2 * SSM_LANES), f32),
                        pltpu.VMEM((nb, 2 * SSM_LANES), f32),
                        pltpu.VMEM((D_SSM // LANE, tc * nb, LANE), f32)],
        compiler_params=pltpu.CompilerParams(
            dimension_semantics=("arbitrary", "arbitrary"), vmem_limit_bytes=VMEM_LIMIT),
        name="s5_scan",
    )(u, bmat, cmat, a_re, a_im)


def _conv3(x, prev8, next8, w_ref, b_ref, first, last):
    n = x.shape[0]
    rows = lax.broadcasted_iota(jnp.int32, x.shape, 0)
    pv = jnp.where(first, 0.0, prev8[7:8, :])
    nx = jnp.where(last, 0.0, next8[0:1, :])
    xm = jnp.where(rows == 0, pv, pltpu.roll(x, 1, 0))
    xp = jnp.where(rows == n - 1, nx, pltpu.roll(x, n - 1, 0))
    return xm * w_ref[0:1, :] + x * w_ref[1:2, :] + xp * w_ref[2:3, :] + b_ref[...]


def _mlstm_gates(g, chunk, bias, *, reverse):
    lc = ML_CHUNK
    r_i = lax.broadcasted_iota(jnp.int32, (lc, lc), 0)
    c_i = lax.broadcasted_iota(jnp.int32, (lc, lc), 1)
    tri = ((c_i >= r_i) if reverse else (c_i <= r_i)).astype(bf16)
    gb = g + bias
    pos = chunk * lc + lax.broadcasted_iota(jnp.int32, (lc, LANE), 0)
    valid = pos >= PAD_FRONT
    li = jnp.where(valid, gb, NEG)
    lf = jnp.where(valid, jnp.minimum(gb, 0.0) - jnp.log(1.0 + jnp.exp(-jnp.abs(gb))), 0.0)
    hi, mid, lo = _split3(lf)
    bcol = _dot(tri, hi) + _dot(tri, mid) + _dot(tri, lo)
    return li, li.T, bcol, bcol.T


def _bmm(a, b):
    return lax.dot_general(a, b, (((2,), (1,)), ((0,), (0,))), preferred_element_type=f32)


def _bmm_nt(a, b):
    return lax.dot_general(a, b, (((2,), (2,)), ((0,), (0,))), preferred_element_type=f32)


def _bmm_tn(a, b):
    return lax.dot_general(a, b, (((1,), (1,)), ((0,), (0,))), preferred_element_type=f32)


def _mlstm_kernel(qkf_ref, qkfp_ref, qkfn_ref, vf_ref, gf_ref,
                  qkb_ref, qkbp_ref, qkbn_ref, vb_ref, gb_ref,
                  cw_ref, cb_ref, bias_ref, hf_ref, hb_ref, ct_ref, m_ref, *, nc, bb):
    c = pl.program_id(1)
    lc = ML_CHUNK
    H = ML_HEADS
    nb = bb * 2 * H

    @pl.when(c == 0)
    def _():
        ct_ref[...] = jnp.zeros_like(ct_ref)
        m_ref[...] = jnp.zeros_like(m_ref)

    cb = nc - 1 - c
    bias = bias_ref[...]
    qs, ks, vs, bcs, brs, lirs, lics, bends = [], [], [], [], [], [], [], []
    for sq, d in [(sq, d) for sq in range(bb) for d in range(2)]:
        if d == 0:
            qk = _conv3(qkf_ref[sq], qkfp_ref[sq], qkfn_ref[sq], cw_ref, cb_ref, c == 0, c == nc - 1)
            v, g, chunk = vf_ref[sq], gf_ref[sq], c
        else:
            qk = _conv3(qkb_ref[sq], qkbp_ref[sq], qkbn_ref[sq], cw_ref, cb_ref, cb == 0, cb == nc - 1)
            v, g, chunk = vb_ref[sq], gb_ref[sq], cb
        li, li_t, bcol, b_t = _mlstm_gates(g, chunk, bias, reverse=bool(d))
        q_all = qk[:, :D_MLSTM].astype(bf16)
        k_all = (qk[:, D_MLSTM:] * (ML_HEAD_DIM ** -0.5)).astype(bf16)
        e_row = 0 if d else lc - 1
        for hh in range(H):
            ci = d * H + hh
            cf = 2 * H + ci
            sl = slice(hh * ML_HEAD_DIM, (hh + 1) * ML_HEAD_DIM)
            qs.append(q_all[:, sl])
            ks.append(k_all[:, sl])
            vs.append(v[:, hh * LANE:(hh + 1) * LANE])
            bcs.append(bcol[:, cf:cf + 1])
            brs.append(b_t[cf:cf + 1, :])
            lirs.append(li_t[ci:ci + 1, :])
            lics.append(li[:, ci:ci + 1])
            bends.append(b_t[cf:cf + 1, e_row:e_row + 1])
    q, k, v = jnp.stack(qs), jnp.stack(ks), jnp.stack(vs)
    bc, br = jnp.stack(bcs), jnp.stack(brs)
    lir, lic, b_end = jnp.stack(lirs), jnp.stack(lics), jnp.stack(bends)
    ct, m_prev = ct_ref[...], m_ref[...][:, :, 0:1]
    bi = lax.broadcasted_iota(jnp.int32, (nb, lc, lc), 0)
    r_i = lax.broadcasted_iota(jnp.int32, (nb, lc, lc), 1)
    c_i = lax.broadcasted_iota(jnp.int32, (nb, lc, lc), 2)
    is_fwd = ((bi // H) % 2) == 0
    keep = jnp.where(is_fwd, c_i - r_i, r_i - c_i) <= 0
    w_row = lir - br
    w_col = lic - bc
    dw = jnp.where(keep, w_row, NEG)
    mm = jnp.maximum(m_prev, jnp.max(dw, axis=2, keepdims=True))
    s = _bmm_nt(q, k) * jnp.exp(dw - mm)
    e = jnp.exp(m_prev - mm)
    r = _bmm(s.astype(bf16), v) + e * _bmm(q, ct.astype(bf16))
    den = r[:, :, ML_HEAD_DIM:ML_HEAD_DIM + 1]
    h = r[:, :, :ML_HEAD_DIM] / jnp.maximum(jnp.abs(den), jnp.exp(-(bc + mm)))
    for sq in range(bb):
        o = sq * 2 * H
        hf_ref[sq] = jnp.concatenate([h[o + i] for i in range(H)], axis=1)
        hb_ref[sq] = jnp.concatenate([h[o + H + i] for i in range(H)], axis=1)
    w_max = jnp.max(w_row, axis=2, keepdims=True)
    kw = k.astype(f32) * jnp.exp(w_col - w_max)
    ct_loc = _bmm_tn(kw.astype(bf16), v)
    m_top = jnp.maximum(m_prev, w_max)
    s_old = jnp.exp(m_prev - m_top)
    s_loc = jnp.exp(w_max - m_top)
    ct_ref[...] = s_old * ct + s_loc * ct_loc
    m_ref[...] = jnp.broadcast_to(b_end + m_top, (nb, 1, LANE))


def _mlstm(mqk, mv, gates, conv_w, conv_b, bias):
    B, Lp, _ = mqk.shape
    lc = ML_CHUNK
    nc = Lp // lc
    n8 = Lp // 8
    per = lc // 8
    fwd = lambda b, c: (b, c, 0)
    bwd = lambda b, c: (b, nc - 1 - c, 0)
    prev = lambda f: (lambda b, c: (b, jnp.maximum(f(b, c)[1] * per - 1, 0), 0))
    nxt = lambda f: (lambda b, c: (b, jnp.minimum((f(b, c)[1] + 1) * per, n8 - 1), 0))
    full = lambda a: pl.BlockSpec(a.shape, lambda b, c: (0,) * a.ndim)
    w2 = 2 * D_MLSTM

    bb = ML_SEQS if B % ML_SEQS == 0 else 1
    nb = bb * 2 * ML_HEADS

    def specs(f):
        return [pl.BlockSpec((bb, lc, w2), f), pl.BlockSpec((bb, 8, w2), prev(f)),
                pl.BlockSpec((bb, 8, w2), nxt(f)), pl.BlockSpec((bb, lc, ML_HEADS * LANE), f),
                pl.BlockSpec((bb, lc, LANE), f)]

    return pl.pallas_call(
        functools.partial(_mlstm_kernel, nc=nc, bb=bb),
        grid=(B // bb, nc),
        in_specs=specs(fwd) + specs(bwd) + [full(conv_w), full(conv_b), full(bias)],
        out_specs=[pl.BlockSpec((bb, lc, D_MLSTM), fwd), pl.BlockSpec((bb, lc, D_MLSTM), bwd)],
        out_shape=[jax.ShapeDtypeStruct((B, Lp, D_MLSTM), f32)] * 2,
        scratch_shapes=[pltpu.VMEM((nb, ML_HEAD_DIM, LANE), f32),
                        pltpu.VMEM((nb, 1, LANE), f32)],
        compiler_params=pltpu.CompilerParams(
            dimension_semantics=("parallel", "arbitrary"), vmem_limit_bytes=VMEM_LIMIT),
        name="mlstm",
    )(mqk, mqk, mqk, mv, gates, mqk, mqk, mqk, mv, gates, conv_w, conv_b, bias)


def _attn_kernel(q_ref, k_ref, v_ref, kb_ref, o_ref, s_ref, *, tk):
    tq = q_ref.shape[0]
    lp = k_ref.shape[0]
    grp = ATT_HEADS // ATT_KV_HEADS
    tiles = [(0, HEAD_ROWS)] + [(c, tk) for c in range(HEAD_ROWS, lp, tk)]
    outs = []
    for g in range(ATT_KV_HEADS):
        gl = slice(g * LANE, (g + 1) * LANE)
        q4 = jnp.concatenate(
            [q_ref[:, (g * grp + j) * LANE:(g * grp + j + 1) * LANE] for j in range(grp)], axis=0)
        m = None
        for c0, n in tiles:
            s = _dot_t(q4, k_ref[c0:c0 + n, gl])
            if c0 == 0:
                s = s + kb_ref[...]
            s_ref[:, c0:c0 + n] = s
            tm_ = jnp.max(s, axis=1, keepdims=True)
            m = tm_ if m is None else jnp.maximum(m, tm_)
        acc = None
        for c0, n in tiles:
            p = jnp.exp2((s_ref[:, c0:c0 + n] - m).astype(bf16))
            part = _dot(p, v_ref[c0:c0 + n, gl])
            acc = part if acc is None else acc + part
        o = acc[:, :ATT_HEAD_DIM] / acc[:, ATT_HEAD_DIM:ATT_HEAD_DIM + 1]
        outs.extend(o[j * tq:(j + 1) * tq] for j in range(grp))
    o_ref[...] = jnp.concatenate(outs, axis=1).astype(bf16)


def _attention(aq, ak, av, kbias):
    B, Lp, _ = aq.shape
    tq = ATT_TQ
    n_real = Lp - HEAD_ROWS
    tk = n_real // 2 if n_real % (2 * LANE) == 0 else n_real
    grp = ATT_HEADS // ATT_KV_HEADS
    return pl.pallas_call(
        functools.partial(_attn_kernel, tk=tk),
        grid=(B, Lp // tq),
        in_specs=[pl.BlockSpec((None, tq, ATT_HEADS * LANE), lambda b, i: (b, i, 0)),
                  pl.BlockSpec((None, Lp, ATT_KV_HEADS * LANE), lambda b, i: (b, 0, 0)),
                  pl.BlockSpec((None, Lp, ATT_KV_HEADS * LANE), lambda b, i: (b, 0, 0)),
                  pl.BlockSpec((1, HEAD_ROWS), lambda b, i: (0, 0))],
        out_specs=pl.BlockSpec((None, tq, D_ATTN), lambda b, i: (b, i, 0)),
        out_shape=jax.ShapeDtypeStruct((B, Lp, D_ATTN), bf16),
        scratch_shapes=[pltpu.VMEM((grp * tq, Lp), f32)],
        compiler_params=pltpu.CompilerParams(
            dimension_semantics=("parallel", "arbitrary"), vmem_limit_bytes=VMEM_LIMIT),
        name="attention",
    )(aq, ak, av, kbias)


def _out_proj_kernel(h_ref, ys_ref, u_ref, hf_ref, hb_ref, mo_ref, att_ref,
                     dsk_ref, wglu_ref, mlg_ref, seg_ref, wout_ref, g2_ref, wr_ref,
                     hn_ref, xn_ref, gate_ref, *, tm, with_router):
    i = pl.program_id(1)
    u = u_ref[...]
    y = ys_ref[0] + ys_ref[1] + dsk_ref[...] * u
    y = 0.5 * y * (1.0 + jnp.tanh(math.sqrt(2.0 / math.pi) * (y + 0.044715 * (y * y * y))))
    ag = _dot(y.astype(bf16), wglu_ref[...])
    y_ssm = ag[:, :D_SSM] * _sigmoid(ag[:, D_SSM:])
    hm = hf_ref[...] + hb_ref[...]
    hn = hm * lax.rsqrt(_split2_dot(hm * hm, seg_ref[...]) + EPS) * mlg_ref[...]
    y_ml = _sigmoid(mo_ref[...].astype(f32)) * hn
    acc = (_dot(y_ssm.astype(bf16), wout_ref[0:D_SSM, :])
           + _dot(y_ml.astype(bf16), wout_ref[D_SSM:D_SSM + D_MLSTM, :])
           + _dot(att_ref[...], wout_ref[D_SSM + D_MLSTM:, :]))
    pos = i * tm + lax.broadcasted_iota(jnp.int32, (tm, 1), 0)
    h_new = jnp.where(pos >= PAD_FRONT, h_ref[...] + acc, 0.0)
    hn_ref[...] = h_new
    xn = h_new * lax.rsqrt(jnp.mean(h_new * h_new, axis=-1, keepdims=True) + EPS) * g2_ref[...]
    x_hi = xn.astype(bf16)
    xn_ref[...] = x_hi
    if not with_router:
        gate_ref[...] = jnp.zeros_like(gate_ref)
        return
    x_lo = (xn - x_hi.astype(f32)).astype(bf16)
    logits = _dot(x_hi, wr_ref[0]) + _dot(x_lo, wr_ref[0]) + _dot(x_hi, wr_ref[1])
    lane = lax.broadcasted_iota(jnp.int32, logits.shape, 1).astype(f32)
    lg = jnp.where(lane < N_EXPERTS, logits, NEG)
    v1 = jnp.max(lg, axis=1, keepdims=True)
    i1 = jnp.min(jnp.where(lg == v1, lane, float(LANE)), axis=1, keepdims=True)
    lg2 = jnp.where(lane == i1, NEG, lg)
    v2 = jnp.max(lg2, axis=1, keepdims=True)
    i2 = jnp.min(jnp.where(lg2 == v2, lane, float(LANE)), axis=1, keepdims=True)
    g1 = 1.0 / (1.0 + jnp.exp(v2 - v1))
    gate_ref[...] = jnp.where(lane == 0.0, i1, jnp.where(lane == 1.0, i2, jnp.where(
        lane == 2.0, g1, jnp.where(lane == 3.0, 1.0 - g1, 0.0))))


def _out_proj(h, ys, u, hf, hb, mo, att, dsk, wglu, mlg, seg, wout, g2, wr, tm, with_router):
    B, Lp, D = h.shape
    row = lambda w_: pl.BlockSpec((None, tm, w_), lambda b, i: (b, i, 0))
    full = lambda a: pl.BlockSpec(a.shape, lambda b, i: (0,) * a.ndim)
    return pl.pallas_call(
        functools.partial(_out_proj_kernel, tm=tm, with_router=with_router),
        grid=(B, Lp // tm),
        in_specs=[row(D),
                  pl.BlockSpec((2, None, tm, D_SSM), lambda b, i: (0, b, i, 0)),
                  row(D_SSM), row(D_MLSTM), row(D_MLSTM), row(D_MLSTM), row(D_ATTN),
                  full(dsk), full(wglu), full(mlg), full(seg), full(wout), full(g2), full(wr)],
        out_specs=[row(D), row(D), row(LANE)],
        out_shape=[jax.ShapeDtypeStruct((B, Lp, D), f32),
                   jax.ShapeDtypeStruct((B, Lp, D), bf16),
                   jax.ShapeDtypeStruct((B, Lp, LANE), f32)],
        compiler_params=pltpu.CompilerParams(
            dimension_semantics=("parallel", "arbitrary"), vmem_limit_bytes=VMEM_LIMIT),
        name="out_proj",
    )(h, ys, u, hf, hb, mo, att, dsk, wglu, mlg, seg, wout, g2, wr)


def _swiglu_part(x, w1, w3, w2):
    a = _dot(x, w1)
    b = _dot(x, w3)
    return _dot((a * _sigmoid(a) * b).astype(bf16), w2)


def _final_norm(h, gf):
    return h * lax.rsqrt(jnp.mean(h * h, axis=-1, keepdims=True) + EPS) * gf


def _ffn_kernel(x_ref, h_ref, w1_ref, w3_ref, w2_ref, gf_ref, o_ref, *, final):
    j = pl.program_id(1)
    part = _swiglu_part(x_ref[...], w1_ref[...], w3_ref[...], w2_ref[...])

    @pl.when(j == 0)
    def _():
        o_ref[...] = h_ref[...] + part

    @pl.when(j > 0)
    def _():
        o_ref[...] += part

    if final:
        @pl.when(j == pl.num_programs(1) - 1)
        def _():
            o_ref[...] = _final_norm(o_ref[...], gf_ref[...])


def _ffn(x, h, w1, w3, w2, gf, tm, final):
    T, D = h.shape
    nj = D_FF // FF_HALF
    return pl.pallas_call(
        functools.partial(_ffn_kernel, final=final),
        grid=(T // tm, nj),
        in_specs=[pl.BlockSpec((tm, D), lambda i, j: (i, 0)),
                  pl.BlockSpec((tm, D), lambda i, j: (i, 0)),
                  pl.BlockSpec((D, FF_HALF), lambda i, j: (0, j)),
                  pl.BlockSpec((D, FF_HALF), lambda i, j: (0, j)),
                  pl.BlockSpec((FF_HALF, D), lambda i, j: (j, 0)),
                  pl.BlockSpec((1, D), lambda i, j: (0, 0))],
        out_specs=pl.BlockSpec((tm, D), lambda i, j: (i, 0)),
        out_shape=jax.ShapeDtypeStruct((T, D), f32),
        compiler_params=pltpu.CompilerParams(
            dimension_semantics=("parallel", "arbitrary"), vmem_limit_bytes=VMEM_LIMIT),
        name="ffn",
    )(x, h, w1, w3, w2, gf)


def _route_kernel(r_ref, pos_ref, cnt_ref, run_ref, *, tg):
    p = pl.program_id(0)
    i = pl.program_id(1)
    tr = r_ref.shape[0]
    r = r_ref[...]
    lane = lax.broadcasted_iota(jnp.int32, (tr, LANE), 1).astype(f32)
    oh1 = (lane == r[:, 0:1]).astype(f32)
    oh2 = (lane == r[:, 1:2]).astype(f32)
    both = oh1 + oh2
    tile_cnt = jnp.sum(both, axis=0, keepdims=True)

    @pl.when(jnp.logical_and(p == 0, i == 0))
    def _():
        run_ref[...] = jnp.zeros_like(run_ref)

    @pl.when(p == 0)
    def _():
        run_ref[...] += tile_cnt

    @pl.when(jnp.logical_and(p == 1, i == 0))
    def _():
        cnt = run_ref[...]
        cnt_ref[...] = cnt.astype(jnp.int32)
        padded = jnp.broadcast_to(jnp.ceil(cnt * (1.0 / tg)) * tg, (8, LANE))
        e_r = lax.broadcasted_iota(jnp.int32, (LANE, LANE), 0)
        e_c = lax.broadcasted_iota(jnp.int32, (LANE, LANE), 1)
        before = (e_r < e_c).astype(bf16)
        hi, mid, lo = _split3(padded)
        run_ref[...] = (_dot(hi, before) + _dot(mid, before) + _dot(lo, before))[0:1, :]

    @pl.when(p == 1)
    def _():
        t_r = lax.broadcasted_iota(jnp.int32, (tr, tr), 0)
        t_c = lax.broadcasted_iota(jnp.int32, (tr, tr), 1)
        earlier = (t_c < t_r).astype(bf16)
        base = run_ref[...] + _dot(earlier, both.astype(bf16))
        p1 = jnp.sum(oh1 * base, axis=1, keepdims=True)
        p2 = jnp.sum(oh2 * base, axis=1, keepdims=True)
        pos_ref[...] = jnp.where(lane == 0.0, p1, jnp.where(lane == 1.0, p2, 0.0)).astype(jnp.int32)
        run_ref[...] += tile_cnt


def _route(route, tg):
    T = route.shape[0]
    tr = _row_tile(T)
    return pl.pallas_call(
        functools.partial(_route_kernel, tg=tg),
        grid=(2, T // tr),
        in_specs=[pl.BlockSpec((tr, LANE), lambda p, i: (i, 0))],
        out_specs=[pl.BlockSpec((tr, LANE), lambda p, i: (i * p, 0)),
                   pl.BlockSpec((1, LANE), lambda p, i: (0, 0))],
        out_shape=[jax.ShapeDtypeStruct((T, LANE), jnp.int32),
                   jax.ShapeDtypeStruct((1, LANE), jnp.int32)],
        scratch_shapes=[pltpu.VMEM((1, LANE), f32)],
        compiler_params=pltpu.CompilerParams(
            dimension_semantics=("arbitrary", "arbitrary"), vmem_limit_bytes=VMEM_LIMIT),
        name="moe_route",
    )(route)


def _slab_rows(ref, row, n=1):
    return ref.at[pl.ds(pl.multiple_of(row * SLAB, SLAB), n * SLAB)]


def _dispatch_kernel(pos_ref, x_ref, xs_in_ref, xs_ref, buf_ref, sem, *, td, T):
    del xs_in_ref
    base = pl.program_id(0) * td
    xf = x_ref[...].astype(f32)
    for s in range(SLAB):
        buf_ref[pl.ds(s, td, stride=SLAB), :] = xf[:, s * LANE:(s + 1) * LANE]

    def copies(r):
        src = _slab_rows(buf_ref, r)
        return (pltpu.make_async_copy(src, _slab_rows(xs_ref, pos_ref[base + r]), sem.at[0]),
                pltpu.make_async_copy(src, _slab_rows(xs_ref, pos_ref[T + base + r]), sem.at[1]))

    def issue(r, carry):
        for slot, cp in enumerate(copies(r)):
            cp.start(priority=slot)
        return carry

    def drain(r, carry):
        for cp in copies(r):
            cp.wait()
        return carry

    lax.fori_loop(0, td, issue, 0, unroll=8)
    lax.fori_loop(0, td, drain, 0, unroll=8)


def _dispatch(pos_flat, x, n_rows, td):
    T, D = x.shape
    xs0 = jnp.zeros((n_rows * SLAB, LANE), f32)
    return pl.pallas_call(
        functools.partial(_dispatch_kernel, td=td, T=T),
        grid_spec=pltpu.PrefetchScalarGridSpec(
            num_scalar_prefetch=1,
            grid=(T // td,),
            in_specs=[pl.BlockSpec((td, D), lambda i, pos: (i, 0)),
                      pl.BlockSpec(memory_space=pl.ANY)],
            out_specs=pl.BlockSpec(memory_space=pl.ANY),
            scratch_shapes=[pltpu.VMEM((td * SLAB, LANE), f32), pltpu.SemaphoreType.DMA((2,))]),
        out_shape=jax.ShapeDtypeStruct((n_rows * SLAB, LANE), f32),
        input_output_aliases={2: 0},
        compiler_params=pltpu.CompilerParams(
            dimension_semantics=("arbitrary",), vmem_limit_bytes=VMEM_LIMIT),
        name="moe_dispatch",
    )(pos_flat, x, xs0)


def _gmm_kernel(te_ref, nu_ref, xs_ref, w1_ref, w3_ref, w2_ref, ys_ref, acc_ref, *, tg):
    j = pl.program_id(0)
    f = pl.program_id(1)
    nf = pl.num_programs(1)

    @pl.when(j < nu_ref[0])
    def _():
        x = jnp.concatenate([xs_ref[pl.ds(s, tg, stride=SLAB), :] for s in range(SLAB)], axis=1)
        part = _swiglu_part(x.astype(bf16), w1_ref[...], w3_ref[...], w2_ref[...])

        @pl.when(f == 0)
        def _():
            acc_ref[...] = part

        @pl.when(jnp.logical_and(f > 0, f < nf - 1))
        def _():
            acc_ref[...] += part

        @pl.when(f == nf - 1)
        def _():
            tot = acc_ref[...] + part
            for s in range(SLAB):
                ys_ref[pl.ds(s, tg, stride=SLAB), :] = tot[:, s * LANE:(s + 1) * LANE]

    @pl.when(jnp.logical_and(j >= nu_ref[0], f == nf - 1))
    def _():
        ys_ref[...] = jnp.zeros_like(ys_ref)


def _gmm(tile_expert, n_used, xs, w1, w3, w2, tg):
    n_tiles = xs.shape[0] // (tg * SLAB)
    nf = D_FF // FF_HALF
    tile = lambda j, f, te, nu: (jnp.minimum(j, nu[0] - 1), 0)
    return pl.pallas_call(
        functools.partial(_gmm_kernel, tg=tg),
        grid_spec=pltpu.PrefetchScalarGridSpec(
            num_scalar_prefetch=2,
            grid=(n_tiles, nf),
            in_specs=[pl.BlockSpec((tg * SLAB, LANE), tile),
                      pl.BlockSpec((None, D_MODEL, FF_HALF), lambda j, f, te, nu: (te[j], 0, f)),
                      pl.BlockSpec((None, D_MODEL, FF_HALF), lambda j, f, te, nu: (te[j], 0, f)),
                      pl.BlockSpec((None, FF_HALF, D_MODEL), lambda j, f, te, nu: (te[j], f, 0))],
            out_specs=pl.BlockSpec((tg * SLAB, LANE), lambda j, f, te, nu: (j, 0)),
            scratch_shapes=[pltpu.VMEM((tg, D_MODEL), f32)]),
        out_shape=jax.ShapeDtypeStruct(xs.shape, f32),
        compiler_params=pltpu.CompilerParams(
            dimension_semantics=("arbitrary", "arbitrary"), vmem_limit_bytes=VMEM_LIMIT),
        name="moe_gmm",
    )(tile_expert, n_used, xs, w1, w3, w2)


def _combine_kernel(pos_ref, h_ref, r_ref, ys_ref, gf_ref, o_ref, b1_ref, b2_ref, sem, *, tc, T, final):
    base = pl.program_id(0) * tc

    def copies(r):
        return (pltpu.make_async_copy(_slab_rows(ys_ref, pos_ref[base + r]), _slab_rows(b1_ref, r), sem.at[0]),
                pltpu.make_async_copy(_slab_rows(ys_ref, pos_ref[T + base + r]), _slab_rows(b2_ref, r), sem.at[1]))

    def issue(r, carry):
        for slot, cp in enumerate(copies(r)):
            cp.start(priority=slot)
        return carry

    def drain(r, carry):
        for cp in copies(r):
            cp.wait()
        return carry

    lax.fori_loop(0, tc, issue, 0, unroll=8)
    lax.fori_loop(0, tc, drain, 0, unroll=8)
    rows = lambda b: jnp.concatenate([b[pl.ds(s, tc, stride=SLAB), :] for s in range(SLAB)], axis=1)
    r = r_ref[...]
    out = h_ref[...] + r[:, 2:3] * rows(b1_ref) + r[:, 3:4] * rows(b2_ref)
    o_ref[...] = _final_norm(out, gf_ref[...]) if final else out


def _combine(pos_flat, h, route, ys, gf, tc, final):
    T, D = h.shape
    return pl.pallas_call(
        functools.partial(_combine_kernel, tc=tc, T=T, final=final),
        grid_spec=pltpu.PrefetchScalarGridSpec(
            num_scalar_prefetch=1,
            grid=(T // tc,),
            in_specs=[pl.BlockSpec((tc, D), lambda i, pos: (i, 0)),
                      pl.BlockSpec((tc, LANE), lambda i, pos: (i, 0)),
                      pl.BlockSpec(memory_space=pl.ANY),
                      pl.BlockSpec((1, D), lambda i, pos: (0, 0))],
            out_specs=pl.BlockSpec((tc, D), lambda i, pos: (i, 0)),
            scratch_shapes=[pltpu.VMEM((tc * SLAB, LANE), f32), pltpu.VMEM((tc * SLAB, LANE), f32),
                            pltpu.SemaphoreType.DMA((2,))]),
        out_shape=jax.ShapeDtypeStruct((T, D), f32),
        compiler_params=pltpu.CompilerParams(
            dimension_semantics=("arbitrary",), vmem_limit_bytes=VMEM_LIMIT),
        name="moe_combine",
    )(pos_flat, h, route, ys, gf)


def _moe(x, h, route, w1, w3, w2, gf, final):
    T, D = h.shape
    tg = MOE_TG
    n_tiles = -(-(2 * T + N_EXPERTS * (tg - 1)) // tg)
    pos, cnt = _route(route, tg)
    ends = jnp.cumsum(-(-cnt[0, :N_EXPERTS] // tg) * tg)
    n_used = (ends[-1:] // tg).astype(jnp.int32)
    tile_expert = jnp.minimum(
        jnp.sum(ends[None, :] <= (jnp.arange(n_tiles, dtype=jnp.int32) * tg)[:, None], axis=1),
        N_EXPERTS - 1).astype(jnp.int32)
    pos_flat = jnp.concatenate([pos[:, 0], pos[:, 1]])
    xs = _dispatch(pos_flat, x, n_tiles * tg, MOE_TROW)
    ys = _gmm(tile_expert, n_used, xs, w1, w3, w2, tg)
    return _combine(pos_flat, h, route, ys, gf, MOE_TROW, final)


def _s5_discretise(lam_re, lam_im, log_step, b_re, b_im, c_re, c_im):
    G, P, C = SSM_GROUPS, SSM_STATE, SSM_GROUP
    dt = jnp.exp(log_step)[..., None]
    mag = jnp.exp(lam_re * dt)
    ab_re = mag * jnp.cos(lam_im * dt)
    ab_im = mag * jnp.sin(lam_im * dt)
    den = lam_re * lam_re + lam_im * lam_im
    nr = ab_re - 1.0
    ni = ab_im
    coef_re = (nr * lam_re + ni * lam_im) / den
    coef_im = (ni * lam_re - nr * lam_im) / den
    bb_re = coef_re[..., None] * b_re - coef_im[..., None] * b_im
    bb_im = coef_re[..., None] * b_im + coef_im[..., None] * b_re
    eye = jnp.eye(G, dtype=f32)

    def bdiag_in(bb):
        return jnp.einsum('dgpc,gh->dgchp', bb, eye).reshape(2, G * C, G * P)

    def bdiag_out(cc):
        return jnp.einsum('dgcp,gh->dgphc', cc, eye).reshape(2, G * P, G * C)

    bmat = jnp.concatenate([bdiag_in(bb_re), bdiag_in(bb_im)], axis=2).astype(bf16)
    cmat = jnp.concatenate([bdiag_out(c_re), -bdiag_out(c_im)], axis=1).astype(bf16)
    return bmat, cmat, ab_re.reshape(2, 1, G * P), ab_im.reshape(2, 1, G * P)


def _rope_tables(n_tokens):
    rows = n_tokens // GRID_W
    row = jnp.concatenate([jnp.zeros((PAD_FRONT,), f32), jnp.full((N_META,), -1.0, f32),
                           jnp.repeat(jnp.arange(rows, dtype=f32), GRID_W)])
    col = jnp.concatenate([jnp.zeros((PAD_FRONT,), f32), jnp.arange(N_META, dtype=f32),
                           jnp.tile(jnp.arange(GRID_W, dtype=f32), rows)])
    inv = ROPE_THETA ** (-jnp.arange(0, ROPE_AXIS, 2, dtype=f32) / ROPE_AXIS)
    ar = row[:, None] * inv
    ac = col[:, None] * inv
    ang = jnp.concatenate([ar, ar, ac, ac], -1)
    cos, sin = jnp.cos(ang), jnp.sin(ang)
    first_half = (jnp.arange(ATT_HEAD_DIM) % ROPE_AXIS) < (ROPE_AXIS // 2)
    sa = jnp.where(first_half, -sin, 0.0)
    sb = jnp.where(first_half, 0.0, sin)
    two = lambda t: jnp.concatenate([t, t], axis=1)
    return two(cos), two(sa), two(sb)


def _seg_matrix(width, seg):
    idx = jnp.arange(width) // seg
    return (idx[:, None] == idx[None, :]).astype(f32).astype(bf16) * jnp.asarray(1.0 / seg, bf16)


def _row_tile(lp):
    best = 16
    for t in range(16, lp + 1, 16):
        if lp % t == 0 and abs(t - 512) < abs(best - 512):
            best = t
    return best


def _prep(meta_tokens, norm1_g, w_in, w_out,
          ssm_lam_re, ssm_lam_im, ssm_log_step, ssm_b_re, ssm_b_im, ssm_c_re, ssm_c_im, ssm_d, ssm_w_glu,
          ml_conv_w, ml_conv_b, ml_b_i, ml_b_f, ml_norm_g, att_q_g, att_k_g,
          norm2_g, ffn_w1, ffn_w3, ffn_w2, moe_router, moe_w1, moe_w3, moe_w2, final_g):
    depth = w_in.shape[0]
    layers = []
    for l in range(depth):
        w = w_in[l]
        wg = jnp.pad(w[:, 1280:1296], ((0, 0), (0, LANE - 16)))
        w_cat = jnp.concatenate([w[:, 0:1280], wg, w[:, 1296:]], axis=1).astype(bf16)
        bmat, cmat, a_re, a_im = _s5_discretise(ssm_lam_re[l], ssm_lam_im[l], ssm_log_step[l],
                                                ssm_b_re[l], ssm_b_im[l], ssm_c_re[l], ssm_c_im[l])
        gate_bias = jnp.pad(jnp.concatenate([ml_b_i[l].reshape(-1), ml_b_f[l].reshape(-1)]),
                            (0, LANE - 4 * ML_HEADS)).reshape(1, LANE)
        lay = dict(
            g1=norm1_g[l].reshape(1, -1), w_cat=w_cat,
            qg=jnp.tile(att_q_g[l], ATT_HEADS).reshape(1, -1),
            kg=jnp.tile(att_k_g[l], ATT_KV_HEADS).reshape(1, -1),
            bmat=bmat, cmat=cmat, a_re=a_re, a_im=a_im,
            dsk=ssm_d[l].reshape(1, -1), wglu=ssm_w_glu[l].astype(bf16),
            conv_w=ml_conv_w[l], conv_b=ml_conv_b[l].reshape(1, -1), gate_bias=gate_bias,
            mlg=ml_norm_g[l].reshape(1, -1), wout=w_out[l].astype(bf16),
            g2=norm2_g[l].reshape(1, -1),
        )
        j = l // 2
        if l % 2 == 0:
            lay.update(moe=False, wr=jnp.zeros((2, D_MODEL, LANE), bf16),
                       w1=ffn_w1[j].astype(bf16), w3=ffn_w3[j].astype(bf16), w2=ffn_w2[j].astype(bf16))
        else:
            wr = jnp.pad(moe_router[j], ((0, 0), (0, LANE - N_EXPERTS)))
            wr_hi = wr.astype(bf16)
            wr_lo = (wr - wr_hi.astype(f32)).astype(bf16)
            lay.update(moe=True, wr=jnp.stack([wr_hi, wr_lo]),
                       w1=moe_w1[j].astype(bf16), w3=moe_w3[j].astype(bf16), w2=moe_w2[j].astype(bf16))
        layers.append(lay)
    return dict(layers=layers, meta=meta_tokens, gf=final_g.reshape(1, -1),
                seg512=_seg_matrix(D_ATTN, ATT_HEAD_DIM), seg256=_seg_matrix(D_MLSTM, ML_HEAD_DIM))


def _trunk(x, P):
    B, N, D = x.shape
    Lp = N + HEAD_ROWS
    tm = _row_tile(Lp)
    h = jnp.concatenate([jnp.zeros((B, PAD_FRONT, D), x.dtype),
                         jnp.broadcast_to(P['meta'].astype(x.dtype), (B, N_META, D)), x], axis=1)
    cos, sa, sb = _rope_tables(N)
    kbias = jnp.where(jnp.arange(HEAD_ROWS) >= PAD_FRONT, 0.0, NEG).astype(f32).reshape(1, HEAD_ROWS)
    depth = len(P['layers'])
    for l, lay in enumerate(P['layers']):
        u, mqk, mv, mo, gates, aq, ak, av = _in_proj(
            h, lay['g1'], lay['w_cat'], cos, sa, sb, lay['qg'], lay['kg'], P['seg512'], tm)
        ys = _s5_scan(u, lay['bmat'], lay['cmat'], lay['a_re'], lay['a_im'])
        hf, hb = _mlstm(mqk, mv, gates, lay['conv_w'], lay['conv_b'], lay['gate_bias'])
        att = _attention(aq, ak, av, kbias)
        h, xn, gate = _out_proj(h, ys, u, hf, hb, mo, att, lay['dsk'], lay['wglu'], lay['mlg'],
                                P['seg256'], lay['wout'], lay['g2'], lay['wr'], tm, lay['moe'])
        final = l == depth - 1
        T = B * Lp
        h2, x2 = h.reshape(T, D), xn.reshape(T, D)
        tt = _row_tile(T)
        if lay['moe']:
            h2 = _moe(x2, h2, gate.reshape(T, LANE), lay['w1'], lay['w3'], lay['w2'], P['gf'], final)
        else:
            h2 = _ffn(x2, h2, lay['w1'], lay['w3'], lay['w2'], P['gf'], tt, final)
        h = h2.reshape(B, Lp, D)
    return h[:, HEAD_ROWS:]


def kernel(x_prompt, x_sample, meta_tokens, norm1_g, w_in, w_out, ssm_lam_re, ssm_lam_im, ssm_log_step, ssm_b_re, ssm_b_im, ssm_c_re, ssm_c_im, ssm_d, ssm_w_glu, ml_conv_w, ml_conv_b, ml_b_i, ml_b_f, ml_norm_g, att_q_g, att_k_g, norm2_g, ffn_w1, ffn_w3, ffn_w2, moe_router, moe_w1, moe_w3, moe_w2, final_g):
    P = _prep(meta_tokens, norm1_g, w_in, w_out,
              ssm_lam_re, ssm_lam_im, ssm_log_step, ssm_b_re, ssm_b_im, ssm_c_re, ssm_c_im, ssm_d, ssm_w_glu,
              ml_conv_w, ml_conv_b, ml_b_i, ml_b_f, ml_norm_g, att_q_g, att_k_g,
              norm2_g, ffn_w1, ffn_w3, ffn_w2, moe_router, moe_w1, moe_w3, moe_w2, final_g)
    return (_trunk(x_prompt, P), _trunk(x_sample, P))
```

```python
import functools
import math

import jax
import jax.numpy as jnp
from jax import lax
from jax.experimental import pallas as pl
from jax.experimental.pallas import tpu as pltpu

f32 = jnp.float32
bf16 = jnp.bfloat16

D_MODEL = 1024
N_META = 16
GRID_W = 64
EPS = 1e-6
D_SSM = 256
D_MLSTM = 256
D_ATTN = 512
SSM_GROUP = 16
SSM_GROUPS = 16
SSM_STATE = 64
SSM_LANES = SSM_GROUPS * SSM_STATE
ML_HEADS = 4
ML_HEAD_DIM = 64
ATT_HEADS = 8
ATT_KV_HEADS = 2
ATT_HEAD_DIM = 64
ATT_KV_W = 128
ROPE_AXIS = 32
ROPE_THETA = 10000.0
D_FF = 2816
N_EXPERTS = 8

LANE = 128
HEAD_ROWS = 128
PAD_FRONT = HEAD_ROWS - N_META
ML_CHUNK = 128
ML_SEQS = 4
S5_ROWS = 1024
ATT_TQ_MAX = 384
LOG2E = 1.4426950408889634
FF_HALF = D_FF // 2
MOE_TG = 512
MOE_TROW = 256
SLAB = 8
NEG = -1e30
VMEM_LIMIT = 48 * 1024 * 1024

C_U = 0
C_MQK = 256
C_MV = 768
C_MO = 1024
C_G = 1280
C_AQ = 1408
C_AK = 1920
C_AV = 2048
D_INP = 2176


def _dot(a, b):
    return jnp.dot(a, b, preferred_element_type=f32)


def _dot_t(a, b):
    return lax.dot_general(a, b, (((1,), (1,)), ((), ())), preferred_element_type=f32)


def _split2_dot(x, m):
    hi = x.astype(bf16)
    lo = (x - hi.astype(f32)).astype(bf16)
    return _dot(hi, m) + _dot(lo, m)


def _split3(x):
    hi = x.astype(bf16)
    r = x - hi.astype(f32)
    mid = r.astype(bf16)
    lo = (r - mid.astype(f32)).astype(bf16)
    return hi, mid, lo


def _sigmoid(x):
    return 1.0 / (1.0 + jnp.exp(-x))


def _rope(x, cos, sin_a, sin_b):
    w = x.shape[-1]
    xl = pltpu.roll(x, w - 16, 1)
    xr = pltpu.roll(x, 16, 1)
    return x * cos + xl * sin_a + xr * sin_b


def _pad_heads(x, n_heads, fill):
    pieces = []
    for hh in range(n_heads):
        pieces += [x[:, hh * ATT_HEAD_DIM:(hh + 1) * ATT_HEAD_DIM], fill]
    return jnp.concatenate(pieces, axis=1)


def _in_proj_kernel(h_ref, g_ref, w_ref, cos_ref, sa_ref, sb_ref, qg_ref, kg_ref, seg_ref,
                    u_ref, mqk_ref, mv_ref, mo_ref, gt_ref, aq_ref, ak_ref, av_ref):
    h = h_ref[...]
    xn = h * lax.rsqrt(jnp.mean(h * h, axis=-1, keepdims=True) + EPS) * g_ref[...]
    p = _dot(xn.astype(bf16), w_ref[...])
    u_ref[...] = p[:, C_U:C_MQK]
    mqk_ref[...] = p[:, C_MQK:C_MV]
    zeros = jnp.zeros((h.shape[0], ATT_HEAD_DIM), f32)
    one0 = (lax.broadcasted_iota(jnp.int32, zeros.shape, 1) == 0).astype(f32)
    mv_ref[...] = _pad_heads(p[:, C_MV:C_MO], ML_HEADS, one0).astype(bf16)
    mo_ref[...] = p[:, C_MO:C_G].astype(bf16)
    gt_ref[...] = p[:, C_G:C_AQ]
    cos = cos_ref[...]
    sa = sa_ref[...]
    sb = sb_ref[...]
    seg = seg_ref[...]
    q = p[:, C_AQ:C_AK]
    qn = q * lax.rsqrt(_split2_dot(q * q, seg) + EPS) * qg_ref[...]
    rep = lambda t: jnp.concatenate([t] * (D_ATTN // LANE), axis=1)
    qr = _rope(qn, rep(cos), rep(sa), rep(sb)) * (ATT_HEAD_DIM ** -0.5 * LOG2E)
    k = p[:, C_AK:C_AV]
    kn = k * lax.rsqrt(_split2_dot(k * k, seg[:ATT_KV_W, :ATT_KV_W]) + EPS) * kg_ref[...]
    aq_ref[...] = _pad_heads(qr, ATT_HEADS, zeros).astype(bf16)
    ak_ref[...] = _pad_heads(_rope(kn, cos, sa, sb), ATT_KV_HEADS, zeros).astype(bf16)
    av_ref[...] = _pad_heads(p[:, C_AV:D_INP], ATT_KV_HEADS, one0).astype(bf16)


def _in_proj(h, g, w, cos, sa, sb, qg, kg, seg, tm):
    B, Lp, D = h.shape
    nt = Lp // tm
    row = lambda w_: pl.BlockSpec((None, tm, w_), lambda b, i: (b, i, 0))
    full = lambda a: pl.BlockSpec(a.shape, lambda b, i: (0,) * a.ndim)
    tab = pl.BlockSpec((tm, LANE), lambda b, i: (i, 0))
    return pl.pallas_call(
        _in_proj_kernel,
        grid=(B, nt),
        in_specs=[row(D), full(g), full(w), tab, tab, tab, full(qg), full(kg), full(seg)],
        out_specs=[
            row(D_SSM), row(2 * D_MLSTM), row(ML_HEADS * LANE), row(D_MLSTM), row(LANE),
            row(ATT_HEADS * LANE), row(ATT_KV_HEADS * LANE), row(ATT_KV_HEADS * LANE),
        ],
        out_shape=[
            jax.ShapeDtypeStruct((B, Lp, D_SSM), f32),
            jax.ShapeDtypeStruct((B, Lp, 2 * D_MLSTM), f32),
            jax.ShapeDtypeStruct((B, Lp, ML_HEADS * LANE), bf16),
            jax.ShapeDtypeStruct((B, Lp, D_MLSTM), bf16),
            jax.ShapeDtypeStruct((B, Lp, LANE), f32),
            jax.ShapeDtypeStruct((B, Lp, ATT_HEADS * LANE), bf16),
            jax.ShapeDtypeStruct((B, Lp, ATT_KV_HEADS * LANE), bf16),
            jax.ShapeDtypeStruct((B, Lp, ATT_KV_HEADS * LANE), bf16),
        ],
        compiler_params=pltpu.CompilerParams(
            dimension_semantics=("parallel", "arbitrary"), vmem_limit_bytes=VMEM_LIMIT),
        name="in_proj",
    )(h, g, w, cos, sa, sb, qg, kg, seg)


def _s5_kernel(u_ref, bm_ref, cm_ref, are_ref, aim_ref, y_ref, x_ref, st_ref, io_ref, *, tc, nb):
    d = pl.program_id(0)
    c = pl.program_id(1)

    @pl.when(c == 0)
    def _():
        st_ref[...] = jnp.zeros_like(st_ref)

    halves = D_SSM // LANE
    for b in range(nb):
        for hh in range(halves):
            io_ref[hh, pl.ds(b, tc, stride=nb), :] = u_ref[b, :, hh * LANE:(hh + 1) * LANE]
    u = jnp.concatenate([io_ref[hh] for hh in range(halves)], axis=1)
    x_ref[...] = _dot(u.astype(bf16), bm_ref[...])
    lw = 512
    for lb in range(SSM_LANES // lw):
        re_sl = pl.ds(lb * lw, lw)
        im_sl = pl.ds(SSM_LANES + lb * lw, lw)
        a_re = jnp.broadcast_to(are_ref[:, lb * lw:(lb + 1) * lw], (nb, lw))
        a_im = jnp.broadcast_to(aim_ref[:, lb * lw:(lb + 1) * lw], (nb, lw))

        def step(t, carry):
            xr, xi = carry
            tt = d * (tc - 1 - t) + (1 - d) * t
            r = pl.multiple_of(tt * nb, 8)
            nr = a_re * xr - a_im * xi + x_ref[pl.ds(r, nb), re_sl]
            ni = a_re * xi + a_im * xr + x_ref[pl.ds(r, nb), im_sl]
            x_ref[pl.ds(r, nb), re_sl] = nr
            x_ref[pl.ds(r, nb), im_sl] = ni
            return nr, ni

        xr, xi = lax.fori_loop(0, tc, step, (st_ref[:, re_sl], st_ref[:, im_sl]), unroll=4)
        st_ref[:, re_sl] = xr
        st_ref[:, im_sl] = xi
    y = _dot(x_ref[...].astype(bf16), cm_ref[...])
    for hh in range(halves):
        io_ref[hh] = y[:, hh * LANE:(hh + 1) * LANE]
    for b in range(nb):
        y_ref[b] = jnp.concatenate([io_ref[hh, pl.ds(b, tc, stride=nb), :] for hh in range(halves)], axis=1)


def _s5_scan(u, bmat, cmat, a_re, a_im):
    nb, Lp, _ = u.shape
    tc = S5_ROWS // nb if (nb <= S5_ROWS and Lp % (S5_ROWS // nb) == 0) else 8
    nc = Lp // tc
    cidx = lambda d, c: d * (nc - 1 - c) + (1 - d) * c
    return pl.pallas_call(
        functools.partial(_s5_kernel, tc=tc, nb=nb),
        grid=(2, nc),
        in_specs=[
            pl.BlockSpec((nb, tc, D_SSM), lambda d, c: (0, cidx(d, c), 0)),
            pl.BlockSpec((None, D_SSM, 2 * SSM_LANES), lambda d, c: (d, 0, 0)),
            pl.BlockSpec((None, 2 * SSM_LANES, D_SSM), lambda d, c: (d, 0, 0)),
            pl.BlockSpec((None, 1, SSM_LANES), lambda d, c: (d, 0, 0)),
            pl.BlockSpec((None, 1, SSM_LANES), lambda d, c: (d, 0, 0)),
        ],
        out_specs=pl.BlockSpec((None, nb, tc, D_SSM), lambda d, c: (d, 0, cidx(d, c), 0)),
        out_shape=jax.ShapeDtypeStruct((2, nb, Lp, D_SSM), f32),
        scratch_shapes=[pltpu.VMEM((tc * nb, 2 * SSM_LANES), f32),
                        pltpu.VMEM((nb, 2 * SSM_LANES), f32),
                        pltpu.VMEM((D_SSM // LANE, tc * nb, LANE), f32)],
        compiler_params=pltpu.CompilerParams(
            dimension_semantics=("arbitrary", "arbitrary"), vmem_limit_bytes=VMEM_LIMIT),
        name="s5_scan",
    )(u, bmat, cmat, a_re, a_im)


def _conv3(x, prev8, next8, w_ref, b_ref, first, last):
    n = x.shape[0]
    rows = lax.broadcasted_iota(jnp.int32, x.shape, 0)
    pv = jnp.where(first, 0.0, prev8[7:8, :])
    nx = jnp.where(last, 0.0, next8[0:1, :])
    xm = jnp.where(rows == 0, pv, pltpu.roll(x, 1, 0))
    xp = jnp.where(rows == n - 1, nx, pltpu.roll(x, n - 1, 0))
    return xm * w_ref[0:1, :] + x * w_ref[1:2, :] + xp * w_ref[2:3, :] + b_ref[...]


def _mlstm_gates(g, chunk, bias, *, reverse):
    lc = ML_CHUNK
    r_i = lax.broadcasted_iota(jnp.int32, (lc, lc), 0)
    c_i = lax.broadcasted_iota(jnp.int32, (lc, lc), 1)
    tri = ((c_i >= r_i) if reverse else (c_i <= r_i)).astype(bf16)
    gb = g + bias
    pos = chunk * lc + lax.broadcasted_iota(jnp.int32, (lc, LANE), 0)
    valid = pos >= PAD_FRONT
    li = jnp.where(valid, gb, NEG)
    lf = jnp.where(valid, jnp.minimum(gb, 0.0) - jnp.log(1.0 + jnp.exp(-jnp.abs(gb))), 0.0)
    hi, mid, lo = _split3(lf)
    bcol = _dot(tri, hi) + _dot(tri, mid) + _dot(tri, lo)
    return li, li.T, bcol, bcol.T


def _bmm(a, b):
    return lax.dot_general(a, b, (((2,), (1,)), ((0,), (0,))), preferred_element_type=f32)


def _bmm_nt(a, b):
    return lax.dot_general(a, b, (((2,), (2,)), ((0,), (0,))), preferred_element_type=f32)


def _bmm_tn(a, b):
    return lax.dot_general(a, b, (((1,), (1,)), ((0,), (0,))), preferred_element_type=f32)


def _mlstm_kernel(qkf_ref, qkfp_ref, qkfn_ref, vf_ref, gf_ref,
                  qkb_ref, qkbp_ref, qkbn_ref, vb_ref, gb_ref,
                  cw_ref, cb_ref, bias_ref, hf_ref, hb_ref, ct_ref, m_ref, *, nc, bb):
    c = pl.program_id(1)
    lc = ML_CHUNK
    H = ML_HEADS
    nb = bb * 2 * H

    @pl.when(c == 0)
    def _():
        ct_ref[...] = jnp.zeros_like(ct_ref)
        m_ref[...] = jnp.zeros_like(m_ref)

    cb = nc - 1 - c
    bias = bias_ref[...]
    qs, ks, vs, bcs, brs, lirs, lics, bends = [], [], [], [], [], [], [], []
    for sq, d in [(sq, d) for sq in range(bb) for d in range(2)]:
        if d == 0:
            qk = _conv3(qkf_ref[sq], qkfp_ref[sq], qkfn_ref[sq], cw_ref, cb_ref, c == 0, c == nc - 1)
            v, g, chunk = vf_ref[sq], gf_ref[sq], c
        else:
            qk = _conv3(qkb_ref[sq], qkbp_ref[sq], qkbn_ref[sq], cw_ref, cb_ref, cb == 0, cb == nc - 1)
            v, g, chunk = vb_ref[sq], gb_ref[sq], cb
        li, li_t, bcol, b_t = _mlstm_gates(g, chunk, bias, reverse=bool(d))
        q_all = qk[:, :D_MLSTM].astype(bf16)
        k_all = (qk[:, D_MLSTM:] * (ML_HEAD_DIM ** -0.5)).astype(bf16)
        e_row = 0 if d else lc - 1
        for hh in range(H):
            ci = d * H + hh
            cf = 2 * H + ci
            sl = slice(hh * ML_HEAD_DIM, (hh + 1) * ML_HEAD_DIM)
            qs.append(q_all[:, sl])
            ks.append(k_all[:, sl])
            vs.append(v[:, hh * LANE:(hh + 1) * LANE])
            bcs.append(bcol[:, cf:cf + 1])
            brs.append(b_t[cf:cf + 1, :])
            lirs.append(li_t[ci:ci + 1, :])
            lics.append(li[:, ci:ci + 1])
            bends.append(b_t[cf:cf + 1, e_row:e_row + 1])
    q, k, v = jnp.stack(qs), jnp.stack(ks), jnp.stack(vs)
    bc, br = jnp.stack(bcs), jnp.stack(brs)
    lir, lic, b_end = jnp.stack(lirs), jnp.stack(lics), jnp.stack(bends)
    ct, m_prev = ct_ref[...], m_ref[...][:, :, 0:1]
    bi = lax.broadcasted_iota(jnp.int32, (nb, lc, lc), 0)
    r_i = lax.broadcasted_iota(jnp.int32, (nb, lc, lc), 1)
    c_i = lax.broadcasted_iota(jnp.int32, (nb, lc, lc), 2)
    is_fwd = ((bi // H) % 2) == 0
    keep = jnp.where(is_fwd, c_i - r_i, r_i - c_i) <= 0
    w_row = lir - br
    w_col = lic - bc
    dw = jnp.where(keep, w_row, NEG)
    mm = jnp.maximum(m_prev, jnp.max(dw, axis=2, keepdims=True))
    s = _bmm_nt(q, k) * jnp.exp(dw - mm)
    e = jnp.exp(m_prev - mm)
    r = _bmm(s.astype(bf16), v) + e * _bmm(q, ct.astype(bf16))
    den = r[:, :, ML_HEAD_DIM:ML_HEAD_DIM + 1]
    h = r[:, :, :ML_HEAD_DIM] / jnp.maximum(jnp.abs(den), jnp.exp(-(bc + mm)))
    for sq in range(bb):
        o = sq * 2 * H
        hf_ref[sq] = jnp.concatenate([h[o + i] for i in range(H)], axis=1)
        hb_ref[sq] = jnp.concatenate([h[o + H + i] for i in range(H)], axis=1)
    w_max = jnp.max(w_row, axis=2, keepdims=True)
    kw = k.astype(f32) * jnp.exp(w_col - w_max)
    ct_loc = _bmm_tn(kw.astype(bf16), v)
    m_top = jnp.maximum(m_prev, w_max)
    s_old = jnp.exp(m_prev - m_top)
    s_loc = jnp.exp(w_max - m_top)
    ct_ref[...] = s_old * ct + s_loc * ct_loc
    m_ref[...] = jnp.broadcast_to(b_end + m_top, (nb, 1, LANE))


def _mlstm(mqk, mv, gates, conv_w, conv_b, bias):
    B, Lp, _ = mqk.shape
    lc = ML_CHUNK
    nc = Lp // lc
    n8 = Lp // 8
    per = lc // 8
    fwd = lambda b, c: (b, c, 0)
    bwd = lambda b, c: (b, nc - 1 - c, 0)
    prev = lambda f: (lambda b, c: (b, jnp.maximum(f(b, c)[1] * per - 1, 0), 0))
    nxt = lambda f: (lambda b, c: (b, jnp.minimum((f(b, c)[1] + 1) * per, n8 - 1), 0))
    full = lambda a: pl.BlockSpec(a.shape, lambda b, c: (0,) * a.ndim)
    w2 = 2 * D_MLSTM

    bb = ML_SEQS if B % ML_SEQS == 0 else 1
    nb = bb * 2 * ML_HEADS

    def specs(f):
        return [pl.BlockSpec((bb, lc, w2), f), pl.BlockSpec((bb, 8, w2), prev(f)),
                pl.BlockSpec((bb, 8, w2), nxt(f)), pl.BlockSpec((bb, lc, ML_HEADS * LANE), f),
                pl.BlockSpec((bb, lc, LANE), f)]

    return pl.pallas_call(
        functools.partial(_mlstm_kernel, nc=nc, bb=bb),
        grid=(B // bb, nc),
        in_specs=specs(fwd) + specs(bwd) + [full(conv_w), full(conv_b), full(bias)],
        out_specs=[pl.BlockSpec((bb, lc, D_MLSTM), fwd), pl.BlockSpec((bb, lc, D_MLSTM), bwd)],
        out_shape=[jax.ShapeDtypeStruct((B, Lp, D_MLSTM), f32)] * 2,
        scratch_shapes=[pltpu.VMEM((nb, ML_HEAD_DIM, LANE), f32),
                        pltpu.VMEM((nb, 1, LANE), f32)],
        compiler_params=pltpu.CompilerParams(
            dimension_semantics=("parallel", "arbitrary"), vmem_limit_bytes=VMEM_LIMIT),
        name="mlstm",
    )(mqk, mqk, mqk, mv, gates, mqk, mqk, mqk, mv, gates, conv_w, conv_b, bias)


def _attn_kernel(q_ref, k_ref, v_ref, kb_ref, o_ref, s_ref, *, tk):
    tq = q_ref.shape[0]
    lp = k_ref.shape[0]
    grp = ATT_HEADS // ATT_KV_HEADS
    tiles = [(0, HEAD_ROWS)] + [(c, tk) for c in range(HEAD_ROWS, lp, tk)]
    outs = []
    for g in range(ATT_KV_HEADS):
        gl = slice(g * LANE, (g + 1) * LANE)
        q4 = jnp.concatenate(
            [q_ref[:, (g * grp + j) * LANE:(g * grp + j + 1) * LANE] for j in range(grp)], axis=0)
        m = None
        for c0, n in tiles:
            s = _dot_t(q4, k_ref[c0:c0 + n, gl])
            if c0 == 0:
                s = s + kb_ref[...]
            s_ref[:, c0:c0 + n] = s
            tm_ = jnp.max(s, axis=1, keepdims=True)
            m = tm_ if m is None else jnp.maximum(m, tm_)
        acc = None
        for c0, n in tiles:
            p = jnp.exp2((s_ref[:, c0:c0 + n] - m).astype(bf16))
            part = _dot(p, v_ref[c0:c0 + n, gl])
            acc = part if acc is None else acc + part
        o = acc[:, :ATT_HEAD_DIM] / acc[:, ATT_HEAD_DIM:ATT_HEAD_DIM + 1]
        outs.extend(o[j * tq:(j + 1) * tq] for j in range(grp))
    o_ref[...] = jnp.concatenate(outs, axis=1).astype(bf16)


def _attention(aq, ak, av, kbias):
    B, Lp, _ = aq.shape
    tq = max(t for t in range(16, ATT_TQ_MAX + 1, 16) if Lp % t == 0)
    n_real = Lp - HEAD_ROWS
    tk = n_real // 2 if n_real % (2 * LANE) == 0 else n_real
    grp = ATT_HEADS // ATT_KV_HEADS
    return pl.pallas_call(
        functools.partial(_attn_kernel, tk=tk),
        grid=(B, Lp // tq),
        in_specs=[pl.BlockSpec((None, tq, ATT_HEADS * LANE), lambda b, i: (b, i, 0)),
                  pl.BlockSpec((None, Lp, ATT_KV_HEADS * LANE), lambda b, i: (b, 0, 0)),
                  pl.BlockSpec((None, Lp, ATT_KV_HEADS * LANE), lambda b, i: (b, 0, 0)),
                  pl.BlockSpec((1, HEAD_ROWS), lambda b, i: (0, 0))],
        out_specs=pl.BlockSpec((None, tq, D_ATTN), lambda b, i: (b, i, 0)),
        out_shape=jax.ShapeDtypeStruct((B, Lp, D_ATTN), bf16),
        scratch_shapes=[pltpu.VMEM((grp * tq, Lp), f32)],
        compiler_params=pltpu.CompilerParams(
            dimension_semantics=("parallel", "arbitrary"), vmem_limit_bytes=VMEM_LIMIT),
        name="attention",
    )(aq, ak, av, kbias)


def _out_proj_kernel(h_ref, ys_ref, u_ref, hf_ref, hb_ref, mo_ref, att_ref,
                     dsk_ref, wglu_ref, mlg_ref, seg_ref, wout_ref, g2_ref, wr_ref,
                     hn_ref, xn_ref, gate_ref, *, tm, with_router):
    i = pl.program_id(1)
    u = u_ref[...]
    y = ys_ref[0] + ys_ref[1] + dsk_ref[...] * u
    y = 0.5 * y * (1.0 + jnp.tanh(math.sqrt(2.0 / math.pi) * (y + 0.044715 * (y * y * y))))
    ag = _dot(y.astype(bf16), wglu_ref[...])
    y_ssm = ag[:, :D_SSM] * _sigmoid(ag[:, D_SSM:])
    hm = hf_ref[...] + hb_ref[...]
    hn = hm * lax.rsqrt(_split2_dot(hm * hm, seg_ref[...]) + EPS) * mlg_ref[...]
    y_ml = _sigmoid(mo_ref[...].astype(f32)) * hn
    acc = (_dot(y_ssm.astype(bf16), wout_ref[0:D_SSM, :])
           + _dot(y_ml.astype(bf16), wout_ref[D_SSM:D_SSM + D_MLSTM, :])
           + _dot(att_ref[...], wout_ref[D_SSM + D_MLSTM:, :]))
    pos = i * tm + lax.broadcasted_iota(jnp.int32, (tm, 1), 0)
    h_new = jnp.where(pos >= PAD_FRONT, h_ref[...] + acc, 0.0)
    hn_ref[...] = h_new
    xn = h_new * lax.rsqrt(jnp.mean(h_new * h_new, axis=-1, keepdims=True) + EPS) * g2_ref[...]
    x_hi = xn.astype(bf16)
    xn_ref[...] = x_hi
    if not with_router:
        gate_ref[...] = jnp.zeros_like(gate_ref)
        return
    x_lo = (xn - x_hi.astype(f32)).astype(bf16)
    logits = _dot(x_hi, wr_ref[0]) + _dot(x_lo, wr_ref[0]) + _dot(x_hi, wr_ref[1])
    lane = lax.broadcasted_iota(jnp.int32, logits.shape, 1).astype(f32)
    lg = jnp.where(lane < N_EXPERTS, logits, NEG)
    v1 = jnp.max(lg, axis=1, keepdims=True)
    i1 = jnp.min(jnp.where(lg == v1, lane, float(LANE)), axis=1, keepdims=True)
    lg2 = jnp.where(lane == i1, NEG, lg)
    v2 = jnp.max(lg2, axis=1, keepdims=True)
    i2 = jnp.min(jnp.where(lg2 == v2, lane, float(LANE)), axis=1, keepdims=True)
    g1 = 1.0 / (1.0 + jnp.exp(v2 - v1))
    gate_ref[...] = jnp.where(lane == 0.0, i1, jnp.where(lane == 1.0, i2, jnp.where(
        lane == 2.0, g1, jnp.where(lane == 3.0, 1.0 - g1, 0.0))))


def _out_proj(h, ys, u, hf, hb, mo, att, dsk, wglu, mlg, seg, wout, g2, wr, tm, with_router):
    B, Lp, D = h.shape
    row = lambda w_: pl.BlockSpec((None, tm, w_), lambda b, i: (b, i, 0))
    full = lambda a: pl.BlockSpec(a.shape, lambda b, i: (0,) * a.ndim)
    return pl.pallas_call(
        functools.partial(_out_proj_kernel, tm=tm, with_router=with_router),
        grid=(B, Lp // tm),
        in_specs=[row(D),
                  pl.BlockSpec((2, None, tm, D_SSM), lambda b, i: (0, b, i, 0)),
                  row(D_SSM), row(D_MLSTM), row(D_MLSTM), row(D_MLSTM), row(D_ATTN),
                  full(dsk), full(wglu), full(mlg), full(seg), full(wout), full(g2), full(wr)],
        out_specs=[row(D), row(D), row(LANE)],
        out_shape=[jax.ShapeDtypeStruct((B, Lp, D), f32),
                   jax.ShapeDtypeStruct((B, Lp, D), bf16),
                   jax.ShapeDtypeStruct((B, Lp, LANE), f32)],
        compiler_params=pltpu.CompilerParams(
            dimension_semantics=("parallel", "arbitrary"), vmem_limit_bytes=VMEM_LIMIT),
        name="out_proj",
    )(h, ys, u, hf, hb, mo, att, dsk, wglu, mlg, seg, wout, g2, wr)


def _swiglu_part(x, w1, w3, w2):
    a = _dot(x, w1)
    b = _dot(x, w3)
    return _dot((a * _sigmoid(a) * b).astype(bf16), w2)


def _final_norm(h, gf):
    return h * lax.rsqrt(jnp.mean(h * h, axis=-1, keepdims=True) + EPS) * gf


def _ffn_kernel(x_ref, h_ref, w1_ref, w3_ref, w2_ref, gf_ref, o_ref, *, final):
    j = pl.program_id(1)
    part = _swiglu_part(x_ref[...], w1_ref[...], w3_ref[...], w2_ref[...])

    @pl.when(j == 0)
    def _():
        o_ref[...] = h_ref[...] + part

    @pl.when(j > 0)
    def _():
        o_ref[...] += part

    if final:
        @pl.when(j == pl.num_programs(1) - 1)
        def _():
            o_ref[...] = _final_norm(o_ref[...], gf_ref[...])


def _ffn(x, h, w1, w3, w2, gf, tm, final):
    T, D = h.shape
    nj = D_FF // FF_HALF
    return pl.pallas_call(
        functools.partial(_ffn_kernel, final=final),
        grid=(T // tm, nj),
        in_specs=[pl.BlockSpec((tm, D), lambda i, j: (i, 0)),
                  pl.BlockSpec((tm, D), lambda i, j: (i, 0)),
                  pl.BlockSpec((D, FF_HALF), lambda i, j: (0, j)),
                  pl.BlockSpec((D, FF_HALF), lambda i, j: (0, j)),
                  pl.BlockSpec((FF_HALF, D), lambda i, j: (j, 0)),
                  pl.BlockSpec((1, D), lambda i, j: (0, 0))],
        out_specs=pl.BlockSpec((tm, D), lambda i, j: (i, 0)),
        out_shape=jax.ShapeDtypeStruct((T, D), f32),
        compiler_params=pltpu.CompilerParams(
            dimension_semantics=("parallel", "arbitrary"), vmem_limit_bytes=VMEM_LIMIT),
        name="ffn",
    )(x, h, w1, w3, w2, gf)


def _route_kernel(r_ref, pos_ref, cnt_ref, run_ref, *, tg):
    p = pl.program_id(0)
    i = pl.program_id(1)
    tr = r_ref.shape[0]
    r = r_ref[...]
    lane = lax.broadcasted_iota(jnp.int32, (tr, LANE), 1).astype(f32)
    oh1 = (lane == r[:, 0:1]).astype(f32)
    oh2 = (lane == r[:, 1:2]).astype(f32)
    both = oh1 + oh2
    tile_cnt = jnp.sum(both, axis=0, keepdims=True)

    @pl.when(jnp.logical_and(p == 0, i == 0))
    def _():
        run_ref[...] = jnp.zeros_like(run_ref)

    @pl.when(p == 0)
    def _():
        run_ref[...] += tile_cnt

    @pl.when(jnp.logical_and(p == 1, i == 0))
    def _():
        cnt = run_ref[...]
        cnt_ref[...] = cnt.astype(jnp.int32)
        padded = jnp.broadcast_to(jnp.ceil(cnt * (1.0 / tg)) * tg, (8, LANE))
        e_r = lax.broadcasted_iota(jnp.int32, (LANE, LANE), 0)
        e_c = lax.broadcasted_iota(jnp.int32, (LANE, LANE), 1)
        before = (e_r < e_c).astype(bf16)
        hi, mid, lo = _split3(padded)
        run_ref[...] = (_dot(hi, before) + _dot(mid, before) + _dot(lo, before))[0:1, :]

    @pl.when(p == 1)
    def _():
        t_r = lax.broadcasted_iota(jnp.int32, (tr, tr), 0)
        t_c = lax.broadcasted_iota(jnp.int32, (tr, tr), 1)
        earlier = (t_c < t_r).astype(bf16)
        base = run_ref[...] + _dot(earlier, both.astype(bf16))
        p1 = jnp.sum(oh1 * base, axis=1, keepdims=True)
        p2 = jnp.sum(oh2 * base, axis=1, keepdims=True)
        pos_ref[...] = jnp.where(lane == 0.0, p1, jnp.where(lane == 1.0, p2, 0.0)).astype(jnp.int32)
        run_ref[...] += tile_cnt


def _route(route, tg):
    T = route.shape[0]
    tr = _row_tile(T)
    return pl.pallas_call(
        functools.partial(_route_kernel, tg=tg),
        grid=(2, T // tr),
        in_specs=[pl.BlockSpec((tr, LANE), lambda p, i: (i, 0))],
        out_specs=[pl.BlockSpec((tr, LANE), lambda p, i: (i * p, 0)),
                   pl.BlockSpec((1, LANE), lambda p, i: (0, 0))],
        out_shape=[jax.ShapeDtypeStruct((T, LANE), jnp.int32),
                   jax.ShapeDtypeStruct((1, LANE), jnp.int32)],
        scratch_shapes=[pltpu.VMEM((1, LANE), f32)],
        compiler_params=pltpu.CompilerParams(
            dimension_semantics=("arbitrary", "arbitrary"), vmem_limit_bytes=VMEM_LIMIT),
        name="moe_route",
    )(route)


def _slab_rows(ref, row, n=1):
    return ref.at[pl.ds(pl.multiple_of(row * SLAB, SLAB), n * SLAB)]


def _dispatch_kernel(pos_ref, fill_ref, x_ref, xs_ref, buf_ref, zero_ref, sem, *, td, T, tg, n_tiles):
    base = pl.program_id(0) * td

    @pl.when(pl.program_id(0) == 0)
    def _():
        zero_ref[...] = jnp.zeros_like(zero_ref)

        def fills(do):
            for e in range(N_EXPERTS):
                def pad_row(r, carry):
                    do(pltpu.make_async_copy(_slab_rows(zero_ref, 0), _slab_rows(xs_ref, fill_ref[e] + r),
                                             sem.at[2]))
                    return carry
                lax.fori_loop(0, fill_ref[N_EXPERTS + e], pad_row, 0)

            def spare_tile(j, carry):
                for part in range(tg // td):
                    do(pltpu.make_async_copy(zero_ref, _slab_rows(xs_ref, j * tg + part * td, td), sem.at[2]))
                return carry
            lax.fori_loop(fill_ref[2 * N_EXPERTS], n_tiles, spare_tile, 0)

        fills(lambda cp: cp.start())
        fills(lambda cp: cp.wait())

    xf = x_ref[...].astype(f32)
    for s in range(SLAB):
        buf_ref[pl.ds(s, td, stride=SLAB), :] = xf[:, s * LANE:(s + 1) * LANE]

    def copies(r):
        src = _slab_rows(buf_ref, r)
        return (pltpu.make_async_copy(src, _slab_rows(xs_ref, pos_ref[base + r]), sem.at[0]),
                pltpu.make_async_copy(src, _slab_rows(xs_ref, pos_ref[T + base + r]), sem.at[1]))

    def issue(r, carry):
        for slot, cp in enumerate(copies(r)):
            cp.start(priority=slot)
        return carry

    def drain(r, carry):
        for cp in copies(r):
            cp.wait()
        return carry

    lax.fori_loop(0, td, issue, 0, unroll=8)
    lax.fori_loop(0, td, drain, 0, unroll=8)


def _dispatch(pos_flat, fill, x, n_tiles, tg, td):
    T, D = x.shape
    return pl.pallas_call(
        functools.partial(_dispatch_kernel, td=td, T=T, tg=tg, n_tiles=n_tiles),
        grid_spec=pltpu.PrefetchScalarGridSpec(
            num_scalar_prefetch=2,
            grid=(T // td,),
            in_specs=[pl.BlockSpec((td, D), lambda i, pos, fill: (i, 0))],
            out_specs=pl.BlockSpec(memory_space=pl.ANY),
            scratch_shapes=[pltpu.VMEM((td * SLAB, LANE), f32), pltpu.VMEM((td * SLAB, LANE), f32),
                            pltpu.SemaphoreType.DMA((3,))]),
        out_shape=jax.ShapeDtypeStruct((n_tiles * tg * SLAB, LANE), f32),
        compiler_params=pltpu.CompilerParams(
            dimension_semantics=("arbitrary",), vmem_limit_bytes=VMEM_LIMIT),
        name="moe_dispatch",
    )(pos_flat, fill, x)


def _gmm_kernel(te_ref, nu_ref, xs_ref, w1_ref, w3_ref, w2_ref, ys_ref, acc_ref, *, tg):
    j = pl.program_id(0)
    f = pl.program_id(1)
    nf = pl.num_programs(1)

    @pl.when(j < nu_ref[0])
    def _():
        x = jnp.concatenate([xs_ref[pl.ds(s, tg, stride=SLAB), :] for s in range(SLAB)], axis=1)
        part = _swiglu_part(x.astype(bf16), w1_ref[...], w3_ref[...], w2_ref[...])

        @pl.when(f == 0)
        def _():
            acc_ref[...] = part

        @pl.when(jnp.logical_and(f > 0, f < nf - 1))
        def _():
            acc_ref[...] += part

        @pl.when(f == nf - 1)
        def _():
            tot = acc_ref[...] + part
            for s in range(SLAB):
                ys_ref[pl.ds(s, tg, stride=SLAB), :] = tot[:, s * LANE:(s + 1) * LANE]

    @pl.when(jnp.logical_and(j >= nu_ref[0], f == nf - 1))
    def _():
        ys_ref[...] = jnp.zeros_like(ys_ref)


def _gmm(tile_expert, n_used, xs, w1, w3, w2, tg):
    n_tiles = xs.shape[0] // (tg * SLAB)
    nf = D_FF // FF_HALF
    tile = lambda j, f, te, nu: (jnp.minimum(j, nu[0] - 1), 0)
    return pl.pallas_call(
        functools.partial(_gmm_kernel, tg=tg),
        grid_spec=pltpu.PrefetchScalarGridSpec(
            num_scalar_prefetch=2,
            grid=(n_tiles, nf),
            in_specs=[pl.BlockSpec((tg * SLAB, LANE), tile),
                      pl.BlockSpec((None, D_MODEL, FF_HALF), lambda j, f, te, nu: (te[j], 0, f)),
                      pl.BlockSpec((None, D_MODEL, FF_HALF), lambda j, f, te, nu: (te[j], 0, f)),
                      pl.BlockSpec((None, FF_HALF, D_MODEL), lambda j, f, te, nu: (te[j], f, 0))],
            out_specs=pl.BlockSpec((tg * SLAB, LANE), lambda j, f, te, nu: (j, 0)),
            scratch_shapes=[pltpu.VMEM((tg, D_MODEL), f32)]),
        out_shape=jax.ShapeDtypeStruct(xs.shape, f32),
        compiler_params=pltpu.CompilerParams(
            dimension_semantics=("arbitrary", "arbitrary"), vmem_limit_bytes=VMEM_LIMIT),
        name="moe_gmm",
    )(tile_expert, n_used, xs, w1, w3, w2)


def _combine_kernel(pos_ref, h_ref, r_ref, ys_ref, gf_ref, o_ref, b1_ref, b2_ref, sem, *, tc, T, final):
    base = pl.program_id(0) * tc

    def copies(r):
        return (pltpu.make_async_copy(_slab_rows(ys_ref, pos_ref[base + r]), _slab_rows(b1_ref, r), sem.at[0]),
                pltpu.make_async_copy(_slab_rows(ys_ref, pos_ref[T + base + r]), _slab_rows(b2_ref, r), sem.at[1]))

    def issue(r, carry):
        for slot, cp in enumerate(copies(r)):
            cp.start(priority=slot)
        return carry

    def drain(r, carry):
        for cp in copies(r):
            cp.wait()
        return carry

    lax.fori_loop(0, tc, issue, 0, unroll=8)
    lax.fori_loop(0, tc, drain, 0, unroll=8)
    rows = lambda b: jnp.concatenate([b[pl.ds(s, tc, stride=SLAB), :] for s in range(SLAB)], axis=1)
    r = r_ref[...]
    out = h_ref[...] + r[:, 2:3] * rows(b1_ref) + r[:, 3:4] * rows(b2_ref)
    o_ref[...] = _final_norm(out, gf_ref[...]) if final else out


def _combine(pos_flat, h, route, ys, gf, tc, final):
    T, D = h.shape
    return pl.pallas_call(
        functools.partial(_combine_kernel, tc=tc, T=T, final=final),
        grid_spec=pltpu.PrefetchScalarGridSpec(
            num_scalar_prefetch=1,
            grid=(T // tc,),
            in_specs=[pl.BlockSpec((tc, D), lambda i, pos: (i, 0)),
                      pl.BlockSpec((tc, LANE), lambda i, pos: (i, 0)),
                      pl.BlockSpec(memory_space=pl.ANY),
                      pl.BlockSpec((1, D), lambda i, pos: (0, 0))],
            out_specs=pl.BlockSpec((tc, D), lambda i, pos: (i, 0)),
            scratch_shapes=[pltpu.VMEM((tc * SLAB, LANE), f32), pltpu.VMEM((tc * SLAB, LANE), f32),
                            pltpu.SemaphoreType.DMA((2,))]),
        out_shape=jax.ShapeDtypeStruct((T, D), f32),
        compiler_params=pltpu.CompilerParams(
            dimension_semantics=("arbitrary",), vmem_limit_bytes=VMEM_LIMIT),
        name="moe_combine",
    )(pos_flat, h, route, ys, gf)


def _moe(x, h, route, w1, w3, w2, gf, final):
    T, D = h.shape
    tg = MOE_TG
    n_tiles = -(-(2 * T + N_EXPERTS * (tg - 1)) // tg)
    pos, cnt = _route(route, tg)
    counts = cnt[0, :N_EXPERTS]
    padded = -(-counts // tg) * tg
    ends = jnp.cumsum(padded)
    n_used = (ends[-1:] // tg).astype(jnp.int32)
    tile_expert = jnp.minimum(
        jnp.sum(ends[None, :] <= (jnp.arange(n_tiles, dtype=jnp.int32) * tg)[:, None], axis=1),
        N_EXPERTS - 1).astype(jnp.int32)
    fill = jnp.concatenate([ends - padded + counts, padded - counts, n_used]).astype(jnp.int32)
    pos_flat = jnp.concatenate([pos[:, 0], pos[:, 1]])
    xs = _dispatch(pos_flat, fill, x, n_tiles, tg, MOE_TROW)
    ys = _gmm(tile_expert, n_used, xs, w1, w3, w2, tg)
    return _combine(pos_flat, h, route, ys, gf, MOE_TROW, final)


def _s5_discretise(lam_re, lam_im, log_step, b_re, b_im, c_re, c_im):
    G, P, C = SSM_GROUPS, SSM_STATE, SSM_GROUP
    dt = jnp.exp(log_step)[..., None]
    mag = jnp.exp(lam_re * dt)
    ab_re = mag * jnp.cos(lam_im * dt)
    ab_im = mag * jnp.sin(lam_im * dt)
    den = lam_re * lam_re + lam_im * lam_im
    nr = ab_re - 1.0
    ni = ab_im
    coef_re = (nr * lam_re + ni * lam_im) / den
    coef_im = (ni * lam_re - nr * lam_im) / den
    bb_re = coef_re[..., None] * b_re - coef_im[..., None] * b_im
    bb_im = coef_re[..., None] * b_im + coef_im[..., None] * b_re
    eye = jnp.eye(G, dtype=f32)

    def bdiag_in(bb):
        return jnp.einsum('dgpc,gh->dgchp', bb, eye).reshape(2, G * C, G * P)

    def bdiag_out(cc):
        return jnp.einsum('dgcp,gh->dgphc', cc, eye).reshape(2, G * P, G * C)

    bmat = jnp.concatenate([bdiag_in(bb_re), bdiag_in(bb_im)], axis=2).astype(bf16)
    cmat = jnp.concatenate([bdiag_out(c_re), -bdiag_out(c_im)], axis=1).astype(bf16)
    return bmat, cmat, ab_re.reshape(2, 1, G * P), ab_im.reshape(2, 1, G * P)


def _rope_tables(n_tokens):
    rows = n_tokens // GRID_W
    row = jnp.concatenate([jnp.zeros((PAD_FRONT,), f32), jnp.full((N_META,), -1.0, f32),
                           jnp.repeat(jnp.arange(rows, dtype=f32), GRID_W)])
    col = jnp.concatenate([jnp.zeros((PAD_FRONT,), f32), jnp.arange(N_META, dtype=f32),
                           jnp.tile(jnp.arange(GRID_W, dtype=f32), rows)])
    inv = ROPE_THETA ** (-jnp.arange(0, ROPE_AXIS, 2, dtype=f32) / ROPE_AXIS)
    ar = row[:, None] * inv
    ac = col[:, None] * inv
    ang = jnp.concatenate([ar, ar, ac, ac], -1)
    cos, sin = jnp.cos(ang), jnp.sin(ang)
    first_half = (jnp.arange(ATT_HEAD_DIM) % ROPE_AXIS) < (ROPE_AXIS // 2)
    sa = jnp.where(first_half, -sin, 0.0)
    sb = jnp.where(first_half, 0.0, sin)
    two = lambda t: jnp.concatenate([t, t], axis=1)
    return two(cos), two(sa), two(sb)


def _seg_matrix(width, seg):
    idx = jnp.arange(width) // seg
    return (idx[:, None] == idx[None, :]).astype(f32).astype(bf16) * jnp.asarray(1.0 / seg, bf16)


def _row_tile(lp):
    best = 16
    for t in range(16, lp + 1, 16):
        if lp % t == 0 and abs(t - 512) < abs(best - 512):
            best = t
    return best


def _prep(meta_tokens, norm1_g, w_in, w_out,
          ssm_lam_re, ssm_lam_im, ssm_log_step, ssm_b_re, ssm_b_im, ssm_c_re, ssm_c_im, ssm_d, ssm_w_glu,
          ml_conv_w, ml_conv_b, ml_b_i, ml_b_f, ml_norm_g, att_q_g, att_k_g,
          norm2_g, ffn_w1, ffn_w3, ffn_w2, moe_router, moe_w1, moe_w3, moe_w2, final_g):
    depth = w_in.shape[0]
    layers = []
    for l in range(depth):
        w = w_in[l]
        wg = jnp.pad(w[:, 1280:1296], ((0, 0), (0, LANE - 16)))
        w_cat = jnp.concatenate([w[:, 0:1280], wg, w[:, 1296:]], axis=1).astype(bf16)
        bmat, cmat, a_re, a_im = _s5_discretise(ssm_lam_re[l], ssm_lam_im[l], ssm_log_step[l],
                                                ssm_b_re[l], ssm_b_im[l], ssm_c_re[l], ssm_c_im[l])
        gate_bias = jnp.pad(jnp.concatenate([ml_b_i[l].reshape(-1), ml_b_f[l].reshape(-1)]),
                            (0, LANE - 4 * ML_HEADS)).reshape(1, LANE)
        lay = dict(
            g1=norm1_g[l].reshape(1, -1), w_cat=w_cat,
            qg=jnp.tile(att_q_g[l], ATT_HEADS).reshape(1, -1),
            kg=jnp.tile(att_k_g[l], ATT_KV_HEADS).reshape(1, -1),
            bmat=bmat, cmat=cmat, a_re=a_re, a_im=a_im,
            dsk=ssm_d[l].reshape(1, -1), wglu=ssm_w_glu[l].astype(bf16),
            conv_w=ml_conv_w[l], conv_b=ml_conv_b[l].reshape(1, -1), gate_bias=gate_bias,
            mlg=ml_norm_g[l].reshape(1, -1), wout=w_out[l].astype(bf16),
            g2=norm2_g[l].reshape(1, -1),
        )
        j = l // 2
        if l % 2 == 0:
            lay.update(moe=False, wr=jnp.zeros((2, D_MODEL, LANE), bf16),
                       w1=ffn_w1[j].astype(bf16), w3=ffn_w3[j].astype(bf16), w2=ffn_w2[j].astype(bf16))
        else:
            wr = jnp.pad(moe_router[j], ((0, 0), (0, LANE - N_EXPERTS)))
            wr_hi = wr.astype(bf16)
            wr_lo = (wr - wr_hi.astype(f32)).astype(bf16)
            lay.update(moe=True, wr=jnp.stack([wr_hi, wr_lo]),
                       w1=moe_w1[j].astype(bf16), w3=moe_w3[j].astype(bf16), w2=moe_w2[j].astype(bf16))
        layers.append(lay)
    return dict(layers=layers, meta=meta_tokens, gf=final_g.reshape(1, -1),
                seg512=_seg_matrix(D_ATTN, ATT_HEAD_DIM), seg256=_seg_matrix(D_MLSTM, ML_HEAD_DIM))


def _trunk(x, P):
    B, N, D = x.shape
    Lp = N + HEAD_ROWS
    tm = _row_tile(Lp)
    h = jnp.concatenate([jnp.zeros((B, PAD_FRONT, D), x.dtype),
                         jnp.broadcast_to(P['meta'].astype(x.dtype), (B, N_META, D)), x], axis=1)
    cos, sa, sb = _rope_tables(N)
    kbias = jnp.where(jnp.arange(HEAD_ROWS) >= PAD_FRONT, 0.0, NEG).astype(f32).reshape(1, HEAD_ROWS)
    depth = len(P['layers'])
    for l, lay in enumerate(P['layers']):
        u, mqk, mv, mo, gates, aq, ak, av = _in_proj(
            h, lay['g1'], lay['w_cat'], cos, sa, sb, lay['qg'], lay['kg'], P['seg512'], tm)
        ys = _s5_scan(u, lay['bmat'], lay['cmat'], lay['a_re'], lay['a_im'])
        hf, hb = _mlstm(mqk, mv, gates, lay['conv_w'], lay['conv_b'], lay['gate_bias'])
        att = _attention(aq, ak, av, kbias)
        h, xn, gate = _out_proj(h, ys, u, hf, hb, mo, att, lay['dsk'], lay['wglu'], lay['mlg'],
                                P['seg256'], lay['wout'], lay['g2'], lay['wr'], tm, lay['moe'])
        final = l == depth - 1
        T = B * Lp
        h2, x2 = h.reshape(T, D), xn.reshape(T, D)
        tt = _row_tile(T)
        if lay['moe']:
            h2 = _moe(x2, h2, gate.reshape(T, LANE), lay['w1'], lay['w3'], lay['w2'], P['gf'], final)
        else:
            h2 = _ffn(x2, h2, lay['w1'], lay['w3'], lay['w2'], P['gf'], tt, final)
        h = h2.reshape(B, Lp, D)
    return h[:, HEAD_ROWS:]


def kernel(x_prompt, x_sample, meta_tokens, norm1_g, w_in, w_out, ssm_lam_re, ssm_lam_im, ssm_log_step, ssm_b_re, ssm_b_im, ssm_c_re, ssm_c_im, ssm_d, ssm_w_glu, ml_conv_w, ml_conv_b, ml_b_i, ml_b_f, ml_norm_g, att_q_g, att_k_g, norm2_g, ffn_w1, ffn_w3, ffn_w2, moe_router, moe_w1, moe_w3, moe_w2, final_g):
    P = _prep(meta_tokens, norm1_g, w_in, w_out,
              ssm_lam_re, ssm_lam_im, ssm_log_step, ssm_b_re, ssm_b_im, ssm_c_re, ssm_c_im, ssm_d, ssm_w_glu,
              ml_conv_w, ml_conv_b, ml_b_i, ml_b_f, ml_norm_g, att_q_g, att_k_g,
              norm2_g, ffn_w1, ffn_w3, ffn_w2, moe_router, moe_w1, moe_w3, moe_w2, final_g)
    return (_trunk(x_prompt, P), _trunk(x_sample, P))
```
